```python
import jax, jax.numpy as jnp
from jax import lax
import numpy as np

D_MODEL = 1024
BATCH = 32
SEQ = 256
DEPTH = 2
DEC_BATCH = 8
DEC_SEQ = 4096
PAST_LEN = 512

GRID_W = 64
CHUNK = 128
N_EVEN = (DEPTH + 1) // 2
N_ODD = DEPTH // 2
N_SUB = 3
D_FF = 2816
EPS = 1e-6
ROPE_THETA = 10000.0
A_GROUPS = 4
A_DIM = 128
A_WIDTH = A_GROUPS * A_DIM
B_HEADS = 8
B_KV = 2
HEAD_DIM = 64
B_WIDTH = B_HEADS * HEAD_DIM
IN_EVEN = 2 * A_WIDTH + B_WIDTH + 2 * B_KV * HEAD_DIM
MIX_EVEN = A_WIDTH + B_WIDTH
EVEN_SPLITS = (A_WIDTH, 2 * A_WIDTH, 2 * A_WIDTH + B_WIDTH, 2 * A_WIDTH + B_WIDTH + B_KV * HEAD_DIM)
C_HEADS = 4
C_DK = 128
C_DV = 128
C_WIDTH = C_HEADS * C_DV
N_DIR = 2
N_GATES = N_DIR * 2 * C_HEADS
D_GROUPS = 4
D_DIM = 128
D_WIDTH = D_GROUPS * D_DIM
IN_ODD = 2 * C_HEADS * C_DK + 2 * C_WIDTH + N_GATES + D_WIDTH
MIX_ODD = C_WIDTH + D_WIDTH
ODD_SPLITS = (C_HEADS * C_DK, 2 * C_HEADS * C_DK, 2 * C_HEADS * C_DK + C_WIDTH,
              2 * C_HEADS * C_DK + 2 * C_WIDTH, 2 * C_HEADS * C_DK + 2 * C_WIDTH + N_GATES)

kernel_name = "hybrid_flow_trunk_ctx_prefix_step"


def _rmsnorm(x, gain):
    xf = x.astype(jnp.float32)
    y = xf * lax.rsqrt(jnp.mean(xf * xf, axis=-1, keepdims=True) + EPS)
    return (y * gain.astype(jnp.float32)).astype(x.dtype)


def _swiglu(h, w_in, w_out):
    g, u = jnp.split(h @ w_in, 2, axis=-1)
    return (jax.nn.silu(g) * u) @ w_out


def _grid_rope(n_tokens):
    rows = n_tokens // GRID_W
    r, cidx = jnp.meshgrid(jnp.arange(rows), jnp.arange(GRID_W), indexing="ij")
    r = r.reshape(-1).astype(jnp.float32)
    cidx = cidx.reshape(-1).astype(jnp.float32)
    n_freq = HEAD_DIM // 4
    inv = ROPE_THETA ** (-jnp.arange(n_freq, dtype=jnp.float32) / n_freq)
    ang = jnp.concatenate([r[:, None] * inv, cidx[:, None] * inv], axis=-1)
    ang = jnp.concatenate([ang, ang], axis=-1)
    return jnp.cos(ang), jnp.sin(ang)


def _apply_rope(x, cos, sin):
    xf = x.astype(jnp.float32)
    x1, x2 = jnp.split(xf, 2, axis=-1)
    rot = jnp.concatenate([-x2, x1], axis=-1)
    return (xf * cos[None, :, None, :] + rot * sin[None, :, None, :]).astype(x.dtype)


def _block_attention(q, k, v):
    bsz, t, h, hd = q.shape
    grp = h // B_KV
    nb = t // CHUNK
    qb = q.reshape(bsz, nb, CHUNK, B_KV, grp, hd).transpose(1, 0, 2, 3, 4, 5)
    scale = hd ** -0.5

    def one_block(qblk):
        s = jnp.einsum('bqkgd,bskd->bkgqs', qblk, k, preferred_element_type=jnp.float32) * scale
        p = jax.nn.softmax(s, axis=-1).astype(v.dtype)
        return jnp.einsum('bkgqs,bskd->bqkgd', p, v)

    out = lax.map(one_block, qb)
    return out.transpose(1, 0, 2, 3, 4, 5).reshape(bsz, t, h * hd)


def _chunk_mlp(u, vv, sgu_gain, w_s, b_s):
    bsz, t, _ = u.shape
    nc = t // CHUNK
    vg = _rmsnorm(vv.reshape(bsz, t, A_GROUPS, A_DIM), sgu_gain)
    vg = vg.reshape(bsz, nc, CHUNK, A_GROUPS, A_DIM)
    s = jnp.einsum('gpq,bnqgc->bnpgc', w_s, vg) + b_s.T[None, None, :, :, None]
    return u * s.reshape(bsz, t, A_WIDTH)


def _fourier(d):
    bsz, t, _ = d.shape
    dg = d.reshape(bsz, t, D_GROUPS, D_DIM).astype(jnp.float32)
    f = jnp.fft.fft2(dg, axes=(1, 3), norm="ortho").real
    return f.reshape(bsz, t, D_WIDTH).astype(d.dtype)


def _mlstm_chunkwise(q, k, v, i_pre, log_f, C0, n0, m0):
    bsz, t, h, _ = q.shape
    nc = t // CHUNK

    def chunks(a):
        return jnp.moveaxis(a.reshape((bsz, nc, CHUNK) + a.shape[2:]), 1, 0)

    causal = jnp.tril(jnp.ones((CHUNK, CHUNK), dtype=bool))

    def step(carry, xs):
        C, n, m = carry
        qc, kc, vc, ic, fc = xs
        b = jnp.cumsum(fc, axis=1).transpose(0, 2, 1)
        li = ic.transpose(0, 2, 1)
        log_d = jnp.where(causal, b[..., :, None] - b[..., None, :] + li[..., None, :], -jnp.inf)
        log_inter = b + m[..., None]
        m_t = jnp.maximum(log_inter, jnp.max(log_d, axis=-1))
        w_intra = jnp.exp(log_d - m_t[..., None])
        w_prev = jnp.exp(log_inter - m_t)
        a = w_intra * jnp.einsum('bthd,bshd->bhts', qc, kc)
        num = (jnp.einsum('bhts,bshe->bthe', a, vc)
               + w_prev.transpose(0, 2, 1)[..., None] * jnp.einsum('bhed,bthd->bthe', C, qc))
        den = a.sum(-1) + w_prev * jnp.einsum('bhd,bthd->bht', n, qc)
        denom = jnp.maximum(jnp.abs(den), jnp.exp(-m_t)).transpose(0, 2, 1)[..., None]
        hc = num / denom
        b_last = b[..., -1]
        log_s = b_last[..., None] - b + li
        m_new = jnp.maximum(b_last + m, jnp.max(log_s, axis=-1))
        w_s = jnp.exp(log_s - m_new[..., None])
        w_p = jnp.exp(b_last + m - m_new)
        C_new = w_p[..., None, None] * C + jnp.einsum('bhs,bshe,bshd->bhed', w_s, vc, kc)
        n_new = w_p[..., None] * n + jnp.einsum('bhs,bshd->bhd', w_s, kc)
        return (C_new, n_new, m_new), hc

    (C, n, m), hs = lax.scan(step, (C0, n0, m0),
                             (chunks(q), chunks(k), chunks(v), chunks(i_pre), chunks(log_f)))
    hs = jnp.moveaxis(hs, 0, 1).reshape(bsz, t, h, v.shape[-1])
    return hs, (C, n, m)


def _even_mixer(h, w_in, w_out, w_s, b_s, sgu_gain, q_gain, k_gain, rope, ctx_kv):
    bsz, t, _ = h.shape
    a_u, a_v, q, k, v = jnp.split(h @ w_in, EVEN_SPLITS, axis=-1)
    out_a = _chunk_mlp(jax.nn.gelu(a_u), jax.nn.gelu(a_v), sgu_gain, w_s, b_s)
    q = _rmsnorm(q.reshape(bsz, t, B_HEADS, HEAD_DIM), q_gain)
    k = _rmsnorm(k.reshape(bsz, t, B_KV, HEAD_DIM), k_gain)
    v = v.reshape(bsz, t, B_KV, HEAD_DIM)
    if ctx_kv is None:
        keys, vals = k, v
        new_ctx = (k, v)
    else:
        cos, sin = rope
        q = _apply_rope(q, cos, sin)
        keys = jnp.concatenate([ctx_kv[0].astype(k.dtype), _apply_rope(k, cos, sin)], axis=1)
        vals = jnp.concatenate([ctx_kv[1].astype(v.dtype), v], axis=1)
        new_ctx = None
    out_b = _block_attention(q, keys, vals)
    return jnp.concatenate([out_a, out_b], axis=-1) @ w_out, new_ctx


def _odd_mixer(h, w_in, b_gate, w_out, ctx_state):
    bsz, t, _ = h.shape
    q, k, v, o, g, d = jnp.split(h @ w_in, ODD_SPLITS, axis=-1)
    q = q.reshape(bsz, t, C_HEADS, C_DK).astype(jnp.float32)
    k = k.reshape(bsz, t, C_HEADS, C_DK).astype(jnp.float32) * (C_DK ** -0.5)
    v = v.reshape(bsz, t, C_HEADS, C_DV).astype(jnp.float32)
    g = g.astype(jnp.float32).reshape(bsz, t, N_DIR, 2, C_HEADS) + b_gate.astype(jnp.float32)
    i_pre = g[:, :, :, 0]
    log_f = jax.nn.log_sigmoid(g[:, :, :, 1])
    if ctx_state is None:
        C0 = jnp.zeros((bsz, N_DIR, C_HEADS, C_DV, C_DK), jnp.float32)
        n0 = jnp.zeros((bsz, N_DIR, C_HEADS, C_DK), jnp.float32)
        m0 = jnp.zeros((bsz, N_DIR, C_HEADS), jnp.float32)
    else:
        C0, n0, m0 = (s.astype(jnp.float32) for s in ctx_state)
    h_f, (Cf, nf, mf) = _mlstm_chunkwise(q, k, v, i_pre[:, :, 0], log_f[:, :, 0], C0[:, 0], n0[:, 0], m0[:, 0])
    rev = lambda a: a[:, ::-1]
    h_b, (Cb, nb, mb) = _mlstm_chunkwise(rev(q), rev(k), rev(v), rev(i_pre[:, :, 1]), rev(log_f[:, :, 1]),
                                         C0[:, 1], n0[:, 1], m0[:, 1])
    h_b = rev(h_b)
    out_c = (jax.nn.sigmoid(o.astype(jnp.float32)) * (h_f + h_b).reshape(bsz, t, C_WIDTH)).astype(h.dtype)
    out_d = _fourier(d)
    out = jnp.concatenate([out_c, out_d], axis=-1) @ w_out
    new_state = (jnp.stack([Cf, Cb], axis=1), jnp.stack([nf, nb], axis=1), jnp.stack([mf, mb], axis=1))
    return out, new_state


def setup_inputs(seed: int = 0) -> dict:
    key = jax.random.key(seed)
    ks = jax.random.split(key, 24)
    nrm = lambda kk, shape, s=1.0: jax.random.normal(kk, shape, jnp.float32) * s
    gate_offset = jnp.array([0.0, 3.0], jnp.float32)[:, None]
    return {
        "x_prompt": nrm(ks[0], (BATCH, SEQ, D_MODEL)),
        "x_sample": nrm(ks[1], (DEC_BATCH, DEC_SEQ, D_MODEL)),
        "c": nrm(ks[2], (DEC_BATCH, D_MODEL)),
        "cache_k": nrm(ks[3], (DEC_BATCH, N_EVEN, PAST_LEN, B_KV, HEAD_DIM)),
        "cache_v": nrm(ks[4], (DEC_BATCH, N_EVEN, PAST_LEN, B_KV, HEAD_DIM)),
        "state_C": nrm(ks[5], (DEC_BATCH, N_ODD, N_DIR, C_HEADS, C_DV, C_DK)),
        "state_n": nrm(ks[6], (DEC_BATCH, N_ODD, N_DIR, C_HEADS, C_DK)),
        "state_m": nrm(ks[7], (DEC_BATCH, N_ODD, N_DIR, C_HEADS), 0.1),
        "c_ctx": nrm(ks[8], (D_MODEL,)),
        "norm_gain": 1.0 + nrm(ks[9], (DEPTH, N_SUB, D_MODEL), 0.02),
        "w_mod": nrm(ks[10], (DEPTH, D_MODEL, 3 * N_SUB * D_MODEL), 0.5 * D_MODEL ** -0.5),
        "b_mod": nrm(ks[11], (DEPTH, 3 * N_SUB * D_MODEL), 0.02),
        "ffn_w_in": nrm(ks[12], (DEPTH, 2, D_MODEL, 2 * D_FF), D_MODEL ** -0.5),
        "ffn_w_out": nrm(ks[13], (DEPTH, 2, D_FF, D_MODEL), D_FF ** -0.5),
        "w_in_even": nrm(ks[14], (N_EVEN, D_MODEL, IN_EVEN), D_MODEL ** -0.5),
        "w_out_even": nrm(ks[15], (N_EVEN, MIX_EVEN, D_MODEL), MIX_EVEN ** -0.5),
        "spatial_w": nrm(ks[16], (N_EVEN, A_GROUPS, CHUNK, CHUNK), CHUNK ** -0.5),
        "spatial_b": 1.0 + nrm(ks[17], (N_EVEN, A_GROUPS, CHUNK), 0.02),
        "sgu_gain": 1.0 + nrm(ks[18], (N_EVEN, A_GROUPS, A_DIM), 0.02),
        "q_gain": 1.0 + nrm(ks[19], (N_EVEN, HEAD_DIM), 0.02),
        "k_gain": 1.0 + nrm(ks[20], (N_EVEN, HEAD_DIM), 0.02),
        "w_in_odd": nrm(ks[21], (N_ODD, D_MODEL, IN_ODD), D_MODEL ** -0.5),
        "b_gate_odd": gate_offset + nrm(ks[22], (N_ODD, N_DIR, 2, C_HEADS), 0.1),
        "w_out_odd": nrm(ks[23], (N_ODD, MIX_ODD, D_MODEL), MIX_ODD ** -0.5),
    }


def reference(x_prompt, x_sample, c, cache_k, cache_v, state_C, state_n, state_m, c_ctx,
              norm_gain, w_mod, b_mod, ffn_w_in, ffn_w_out,
              w_in_even, w_out_even, spatial_w, spatial_b, sgu_gain, q_gain, k_gain,
              w_in_odd, b_gate_odd, w_out_odd):

    def layer(l, x, mod, rope, ctx):
        g = norm_gain[l]

        def modulated(j, y):
            return _rmsnorm(y, g[j]) * (1 + mod[:, :, 3 * j + 1]) + mod[:, :, 3 * j]

        x = x + 0.5 * mod[:, :, 2] * _swiglu(modulated(0, x), ffn_w_in[l, 0], ffn_w_out[l, 0])
        h = modulated(1, x)
        e = l // 2
        if l % 2 == 0:
            out, new_ctx = _even_mixer(h, w_in_even[e], w_out_even[e], spatial_w[e], spatial_b[e],
                                       sgu_gain[e], q_gain[e], k_gain[e], rope, ctx)
        else:
            out, new_ctx = _odd_mixer(h, w_in_odd[e], b_gate_odd[e], w_out_odd[e], ctx)
        x = x + mod[:, :, 5] * out
        x = x + 0.5 * mod[:, :, 8] * _swiglu(modulated(2, x), ffn_w_in[l, 1], ffn_w_out[l, 1])
        return x, new_ctx

    x = x_prompt
    ks_, vs_, Cs_, ns_, ms_ = [], [], [], [], []
    for l in range(DEPTH):
        mod = (jax.nn.silu(c_ctx) @ w_mod[l] + b_mod[l]).reshape(1, 1, 3 * N_SUB, D_MODEL)
        x, st = layer(l, x, mod, None, None)
        if l % 2 == 0:
            ks_.append(st[0])
            vs_.append(st[1])
        else:
            Cs_.append(st[0])
            ns_.append(st[1])
            ms_.append(st[2])
    y_prompt = x
    new_cache_k = jnp.stack(ks_, axis=1)
    new_cache_v = jnp.stack(vs_, axis=1)
    new_state_C = jnp.stack(Cs_, axis=1)
    new_state_n = jnp.stack(ns_, axis=1)
    new_state_m = jnp.stack(ms_, axis=1)

    xs = x_sample
    rope = _grid_rope(xs.shape[1])
    for l in range(DEPTH):
        mod = (jax.nn.silu(c) @ w_mod[l] + b_mod[l]).reshape(xs.shape[0], 1, 3 * N_SUB, D_MODEL)
        e = l // 2
        if l % 2 == 0:
            ctx = (cache_k[:, e], cache_v[:, e])
        else:
            ctx = (state_C[:, e], state_n[:, e], state_m[:, e])
        xs, _ = layer(l, xs, mod, rope, ctx)
    y_sample = xs

    return (y_prompt, y_sample, new_cache_k, new_cache_v, new_state_C, new_state_n, new_state_m)
```

```python
import functools
import math

import jax
import jax.numpy as jnp
from jax import lax
from jax.experimental import pallas as pl
from jax.experimental.pallas import tpu as pltpu

BF = jnp.bfloat16
F32 = jnp.float32

EPS = 1e-6
ROPE_THETA = 10000.0
GRID_W = 64
CHUNK = 128
N_SUB = 3
A_GROUPS = 4
A_DIM = 128
A_WIDTH = A_GROUPS * A_DIM
B_HEADS = 8
B_KV = 2
HEAD_DIM = 64
B_WIDTH = B_HEADS * HEAD_DIM
C_HEADS = 4
C_DK = 128
C_DV = 128
C_WIDTH = C_HEADS * C_DV
N_DIR = 2
N_STREAMS = N_DIR * C_HEADS
D_GROUPS = 4
D_DIM = 128
D_WIDTH = D_GROUPS * D_DIM
LANES = 128

VMEM_LIMIT = 56 * 1024 * 1024


def _params(*sem):
    return pltpu.CompilerParams(dimension_semantics=sem, vmem_limit_bytes=VMEM_LIMIT)


def _dot(a, b):
    return jnp.dot(a, b, preferred_element_type=F32)


def _dot_nt(a, b):
    return lax.dot_general(a, b, (((1,), (1,)), ((), ())), preferred_element_type=F32)


def _dot_tn(a, b):
    return lax.dot_general(a, b, (((0,), (0,)), ((), ())), preferred_element_type=F32)


def _sigmoid(x):
    return 1.0 / (1.0 + jnp.exp(-x))


def _gelu_tanh(x):
    return 0.5 * x * (1.0 + jnp.tanh(math.sqrt(2.0 / math.pi) * (x + 0.044715 * (x * x * x))))


def _norm_mod(x, gain, shift, scale):
    ms = jnp.mean(x * x, axis=-1, keepdims=True)
    y = x * lax.rsqrt(ms + EPS) * gain
    return y * (1.0 + scale) + shift


def _const_spec(shape):
    zeros = (0,) * len(shape)
    return pl.BlockSpec(shape, lambda *_: zeros, pipeline_mode=pl.Buffered(1))


def _row_tile(t, target):
    tm = min(t, target)
    assert t % tm == 0
    return tm


def _mod_kernel(c_ref, w_ref, b_ref, o_ref):
    c = c_ref[...]
    sc = (c * _sigmoid(c)).astype(BF)
    o_ref[0] = _dot(sc, w_ref[0].astype(BF)) + b_ref[0]


def _mod_call(c_rows, w_mod, b_mod):
    depth, d, n = w_mod.shape
    rows = c_rows.shape[0]
    tn = 1024
    assert n % tn == 0
    return pl.pallas_call(
        _mod_kernel,
        grid=(depth, n // tn),
        in_specs=[
            pl.BlockSpec((rows, d), lambda l, j: (0, 0)),
            pl.BlockSpec((1, d, tn), lambda l, j: (l, 0, j)),
            pl.BlockSpec((1, 1, tn), lambda l, j: (l, 0, j)),
        ],
        out_specs=pl.BlockSpec((1, rows, tn), lambda l, j: (l, 0, j)),
        out_shape=jax.ShapeDtypeStruct((depth, rows, n), F32),
        compiler_params=_params("parallel", "parallel"),
        name="mod",
    )(c_rows, w_mod, b_mod.reshape(depth, 1, n))


FFN_CHUNK = 256


def _ffn_kernel(*refs, n_mix_a, sub):
    if n_mix_a:
        x_ref = refs[0]
        a_refs = refs[1:1 + n_mix_a]
        b_ref, mod_ref, gain_ref, wa_ref, wb_ref, win_ref, wout_ref, o_ref, act_ref = refs[1 + n_mix_a:]
    else:
        x_ref, mod_ref, gain_ref, win_ref, wout_ref, o_ref, act_ref = refs
    x = x_ref[0]
    if n_mix_a:
        if n_mix_a == 1:
            a = a_refs[0][0]
        else:
            a = (a_refs[0][0].astype(F32) + a_refs[1][0].astype(F32)).astype(BF)
        mix = _dot(a, wa_ref[...]) + _dot(b_ref[0], wb_ref[...])
        x = x + mod_ref[0, 5:6, :] * mix
    h = _norm_mod(x, gain_ref[sub:sub + 1, :], mod_ref[0, 3 * sub:3 * sub + 1, :],
                  mod_ref[0, 3 * sub + 1:3 * sub + 2, :]).astype(BF)
    d_ff = act_ref.shape[1]
    for c in range(d_ff // FFN_CHUNK):
        lo = c * FFN_CHUNK
        g = _dot(h, win_ref[:, lo:lo + FFN_CHUNK])
        u = _dot(h, win_ref[:, d_ff + lo:d_ff + lo + FFN_CHUNK])
        act_ref[:, lo:lo + FFN_CHUNK] = (g * _sigmoid(g) * u).astype(BF)
    y = _dot(act_ref[...], wout_ref[...])
    o_ref[0] = x + (0.5 * mod_ref[0, 3 * sub + 2:3 * sub + 3, :]) * y


def _ffn_call(x, mod, gain, w_in, w_out, sub, mix=None):
    bm, t, d = x.shape
    d_ff = w_out.shape[0]
    tm = _row_tile(t, 512)
    row = lambda w: pl.BlockSpec((1, tm, w), lambda b, i: (b, i, 0))
    in_specs = [row(d)]
    args = [x]
    n_mix_a = 0
    if mix is not None:
        a_list, b_arr, w_a, w_b = mix
        n_mix_a = len(a_list)
        for a in a_list:
            in_specs.append(row(a.shape[-1]))
            args.append(a)
        in_specs.append(row(b_arr.shape[-1]))
        args.append(b_arr)
    in_specs += [pl.BlockSpec((1, 3 * N_SUB, d), lambda b, i: (b, 0, 0)), _const_spec(gain.shape)]
    args += [mod, gain]
    if mix is not None:
        in_specs += [_const_spec(w_a.shape), _const_spec(w_b.shape)]
        args += [w_a, w_b]
    in_specs += [_const_spec(w_in.shape), _const_spec(w_out.shape)]
    args += [w_in, w_out]
    return pl.pallas_call(
        functools.partial(_ffn_kernel, n_mix_a=n_mix_a, sub=sub),
        grid=(bm, t // tm),
        in_specs=in_specs,
        out_specs=row(d),
        out_shape=jax.ShapeDtypeStruct((bm, t, d), F32),
        scratch_shapes=[pltpu.VMEM((tm, d_ff), BF)],
        compiler_params=_params("parallel", "parallel"),
        name="ffn_mix" if mix is not None else "ffn",
    )(*args)


def _rope(x, cos, sin_signed):
    w = x.shape[1]
    n = w // LANES
    cos = jnp.concatenate([cos] * n, axis=1)
    sin_signed = jnp.concatenate([sin_signed] * n, axis=1)
    lane = lax.broadcasted_iota(jnp.int32, (1, w), 1)
    first_half = (lane & (HEAD_DIM - 1)) < (HEAD_DIM // 2)
    fwd = pltpu.roll(x, w - HEAD_DIM // 2, axis=1)
    bwd = pltpu.roll(x, HEAD_DIM // 2, axis=1)
    return x * cos + jnp.where(first_half, fwd, bwd) * sin_signed


def _even_in_kernel(*refs, has_rope, is_ctx):
    it = iter(refs)
    x_ref, mod_ref, gain_ref, w_ref, sgu_ref, ws_ref, bs_ref, qg_ref, kg_ref, bd_ref = (next(it) for _ in range(10))
    if has_rope:
        cos_ref, sin_ref = next(it), next(it)
    oa_ref, q_ref, kd_ref, vd_ref = (next(it) for _ in range(4))
    if is_ctx:
        k32_ref, v32_ref = next(it), next(it)

    tm = x_ref.shape[1]
    nck = tm // CHUNK
    x = x_ref[0]
    h = _norm_mod(x, gain_ref[1:2, :], mod_ref[0, 3:4, :], mod_ref[0, 4:5, :]).astype(BF)

    au = _gelu_tanh(_dot(h, w_ref[:, 0:A_WIDTH]))
    av = _gelu_tanh(_dot(h, w_ref[:, A_WIDTH:2 * A_WIDTH]))
    for g in range(A_GROUPS):
        cols = slice(g * A_DIM, (g + 1) * A_DIM)
        blk = av[:, cols]
        ms = jnp.mean(blk * blk, axis=-1, keepdims=True)
        vg = (blk * lax.rsqrt(ms + EPS) * sgu_ref[:, cols]).astype(BF)
        rhs = jnp.concatenate([vg[c * CHUNK:(c + 1) * CHUNK, :] for c in range(nck)], axis=1)
        s = _dot(ws_ref[g], rhs)
        for c in range(nck):
            rows = slice(c * CHUNK, (c + 1) * CHUNK)
            gate = s[:, c * A_DIM:(c + 1) * A_DIM] + bs_ref[g]
            oa_ref[0, rows, cols] = (au[rows, cols] * gate).astype(BF)

    lane = lax.broadcasted_iota(jnp.int32, (1, LANES), 1)
    lo_half = lane < HEAD_DIM
    if has_rope:
        cos = cos_ref[...]
        sin = sin_ref[...]

    off = 2 * A_WIDTH
    q = _dot(h, w_ref[:, off:off + B_WIDTH])
    ms = _dot((q * q).astype(BF), bd_ref[...])
    q = q * lax.rsqrt(ms + EPS) * qg_ref[...]
    if has_rope:
        q = _rope(q, cos, sin)
    q_ref[0] = (q * (HEAD_DIM ** -0.5)).astype(BF)

    off += B_WIDTH
    kvw = 2 * B_KV * HEAD_DIM
    k = _dot(h, w_ref[:, off:off + kvw])
    ms = _dot((k * k).astype(BF), bd_ref[0:kvw, 0:kvw])
    k = k * lax.rsqrt(ms + EPS) * kg_ref[...]
    if is_ctx:
        k32_ref[0] = jnp.where(lo_half, k[:, 0:LANES], k[:, LANES:2 * LANES])
    if has_rope:
        k = _rope(k, cos, sin)
    kd_ref[0] = k.astype(BF)
    off += kvw
    v = _dot(h, w_ref[:, off:off + kvw])
    if is_ctx:
        v32_ref[0] = jnp.where(lo_half, v[:, 0:LANES], v[:, LANES:2 * LANES])
    vd_ref[0] = v.astype(BF)


def _even_in_call(x, mod, gain, w, sgu, ws, bs, qg, kg, bd, rope, is_ctx):
    bm, t, d = x.shape
    tm = _row_tile(t, 512)
    kvw = 2 * B_KV * HEAD_DIM
    row = lambda w_: pl.BlockSpec((1, tm, w_), lambda b, i: (b, i, 0))
    in_specs = [row(d), pl.BlockSpec((1, 3 * N_SUB, d), lambda b, i: (b, 0, 0))]
    in_specs += [_const_spec(a.shape) for a in (gain, w, sgu, ws, bs, qg, kg, bd)]
    args = [x, mod, gain, w, sgu, ws, bs, qg, kg, bd]
    if rope is not None:
        in_specs += [pl.BlockSpec((tm, LANES), lambda b, i: (i, 0))] * 2
        args += list(rope)
    out_specs = [row(A_WIDTH), row(B_WIDTH), row(kvw), row(kvw)]
    out_shape = [jax.ShapeDtypeStruct((bm, t, A_WIDTH), BF), jax.ShapeDtypeStruct((bm, t, B_WIDTH), BF),
                 jax.ShapeDtypeStruct((bm, t, kvw), BF), jax.ShapeDtypeStruct((bm, t, kvw), BF)]
    if is_ctx:
        out_specs += [row(B_KV * HEAD_DIM)] * 2
        out_shape += [jax.ShapeDtypeStruct((bm, t, B_KV * HEAD_DIM), F32)] * 2
    return pl.pallas_call(
        functools.partial(_even_in_kernel, has_rope=rope is not None, is_ctx=is_ctx),
        grid=(bm, t // tm),
        in_specs=in_specs,
        out_specs=out_specs,
        out_shape=out_shape,
        compiler_params=_params("parallel", "parallel"),
        name="even_in",
    )(*args)


def _attn_kernel(q_ref, kd_ref, vd_ref, o_ref):
    lane = lax.broadcasted_iota(jnp.int32, (1, LANES), 1)
    lo_half = lane < HEAD_DIM
    heads_per_kv = B_HEADS // B_KV
    for p in range(B_WIDTH // LANES):
        cols = slice(p * LANES, (p + 1) * LANES)
        kv = (2 * p) // heads_per_kv
        kcols = slice(kv * LANES, (kv + 1) * LANES)
        qp = q_ref[0, :, cols]
        k2 = kd_ref[0, :, kcols]
        v2 = vd_ref[0, :, kcols]
        outs = []
        for half in range(2):
            keep = lo_half if half == 0 else jnp.logical_not(lo_half)
            qm = jnp.where(keep, qp, jnp.zeros_like(qp))
            s = _dot_nt(qm, k2)
            m = jnp.max(s, axis=-1, keepdims=True)
            e = jnp.exp(s - m)
            l = jnp.sum(e, axis=-1, keepdims=True)
            outs.append(_dot(e.astype(BF), v2) / l)
        o_ref[0, :, cols] = jnp.where(lo_half, outs[0], outs[1]).astype(BF)


def _attn_call(q, kd, vd):
    b, t, w = q.shape
    s = kd.shape[1]
    tq = _row_tile(t, 256)
    return pl.pallas_call(
        _attn_kernel,
        grid=(b, t // tq),
        in_specs=[
            pl.BlockSpec((1, tq, w), lambda i, j: (i, j, 0)),
            pl.BlockSpec((1, s, kd.shape[2]), lambda i, j: (i, 0, 0)),
            pl.BlockSpec((1, s, vd.shape[2]), lambda i, j: (i, 0, 0)),
        ],
        out_specs=pl.BlockSpec((1, tq, w), lambda i, j: (i, j, 0)),
        out_shape=jax.ShapeDtypeStruct((b, t, w), BF),
        compiler_params=_params("parallel", "parallel"),
        name="attn",
    )(q, kd, vd)


def _log_sigmoid(x):
    return jnp.minimum(x, 0.0) - jnp.log(1.0 + jnp.exp(-jnp.abs(x)))


def _odd_in_kernel(x_ref, mod_ref, gain_ref, w_ref, wg_ref, bg_ref, q_ref, k_ref, v_ref, o_ref, d_ref, g_ref):
    x = x_ref[0]
    h = _norm_mod(x, gain_ref[1:2, :], mod_ref[0, 3:4, :], mod_ref[0, 4:5, :]).astype(BF)
    w = C_WIDTH
    q_ref[0] = _dot(h, w_ref[:, 0:w]).astype(BF)
    k_ref[0] = (_dot(h, w_ref[:, w:2 * w]) * (C_DK ** -0.5)).astype(BF)
    v_ref[0] = _dot(h, w_ref[:, 2 * w:3 * w]).astype(BF)
    o_ref[0] = _sigmoid(_dot(h, w_ref[:, 3 * w:4 * w])).astype(BF)
    d_ref[0] = _dot(h, w_ref[:, 4 * w:4 * w + D_WIDTH]).astype(BF)
    g = _dot(h, wg_ref[...]) + bg_ref[...]
    lane = lax.broadcasted_iota(jnp.int32, (1, LANES), 1)
    is_forget = jnp.logical_and(lane >= N_STREAMS, lane < 2 * N_STREAMS)
    g_ref[0] = jnp.where(is_forget, _log_sigmoid(g), g)


def _odd_in_call(x, mod, gain, w, wg, bg):
    bm, t, d = x.shape
    tm = _row_tile(t, 512)
    row = lambda w_: pl.BlockSpec((1, tm, w_), lambda b, i: (b, i, 0))
    widths = [C_WIDTH, C_WIDTH, C_WIDTH, C_WIDTH, D_WIDTH]
    return pl.pallas_call(
        _odd_in_kernel,
        grid=(bm, t // tm),
        in_specs=[row(d), pl.BlockSpec((1, 3 * N_SUB, d), lambda b, i: (b, 0, 0))]
        + [_const_spec(a.shape) for a in (gain, w, wg, bg)],
        out_specs=[row(w_) for w_ in widths] + [row(LANES)],
        out_shape=[jax.ShapeDtypeStruct((bm, t, w_), BF) for w_ in widths]
        + [jax.ShapeDtypeStruct((bm, t, LANES), F32)],
        compiler_params=_params("parallel", "parallel"),
        name="odd_in",
    )(x, mod, gain, w, wg, bg)


def _split3(x):
    hi = x.astype(BF)
    r1 = x - hi.astype(F32)
    mid = r1.astype(BF)
    lo = (r1 - mid.astype(F32)).astype(BF)
    return hi, mid, lo


def _mlstm_kernel(qf_ref, kf_ref, vf_ref, of_ref, gf_ref, qb_ref, kb_ref, vb_ref, ob_ref, gb_ref,
                  c0_ref, n0_ref, m0_ref,
                  hf_ref, hb_ref, c1_ref, n1_ref, m1_ref,
                  c_s, n_s, m_s):
    i = pl.program_id(1)
    tb = qf_ref.shape[1]
    nck = tb // CHUNK

    @pl.when(i == 0)
    def _():
        c_s[...] = c0_ref[0]
        n_s[...] = n0_ref[0]
        m_s[...] = m0_ref[0]

    lane = lax.broadcasted_iota(jnp.int32, (1, LANES), 1)
    is_fwd = lane < C_HEADS
    r_i = lax.broadcasted_iota(jnp.int32, (CHUNK, CHUNK), 0)
    c_i = lax.broadcasted_iota(jnp.int32, (CHUNK, CHUNK), 1)
    lower = c_i <= r_i
    upper = c_i >= r_i
    tri_lo = jnp.where(lower, 1.0, 0.0).astype(BF)
    tri_up = jnp.where(upper, 1.0, 0.0).astype(BF)

    def chunk_body(c, carry):
        rf = pl.ds(pl.multiple_of(c * CHUNK, CHUNK), CHUNK)
        rb = pl.ds(pl.multiple_of((nck - 1 - c) * CHUNK, CHUNK), CHUNK)
        gf = gf_ref[0, rf, :]
        gb = gb_ref[0, rb, :]
        g_i = jnp.where(is_fwd, gf, gb)
        g_f = jnp.where(is_fwd, pltpu.roll(gf, LANES - N_STREAMS, axis=1),
                        pltpu.roll(gb, LANES - N_STREAMS, axis=1))
        parts = _split3(g_f)
        cum_f = sum(_dot(tri_lo, p) for p in parts)
        cum_b = sum(_dot(tri_up, p) for p in parts)
        cum = jnp.where(is_fwd, cum_f, cum_b)
        total = cum_f[CHUNK - 1:CHUNK, :]
        m_row = m_s[...]
        log_inter = cum + m_row
        rt = (g_i - cum).T
        log_s = total - cum + g_i
        m_new = jnp.maximum(total + m_row, jnp.max(log_s, axis=0, keepdims=True))
        w_s_all = jnp.exp(log_s - m_new)
        w_p_all = jnp.exp(total + m_row - m_new)
        m_s[...] = m_new

        for d in range(N_DIR):
            q_ref, k_ref, v_ref, o_ref, h_ref = ((qf_ref, kf_ref, vf_ref, of_ref, hf_ref) if d == 0
                                                 else (qb_ref, kb_ref, vb_ref, ob_ref, hb_ref))
            rows = rf if d == 0 else rb
            mask = lower if d == 0 else upper
            for hh in range(C_HEADS):
                sl = d * C_HEADS + hh
                cols = slice(hh * C_DK, (hh + 1) * C_DK)
                q = q_ref[0, rows, cols]
                k = k_ref[0, rows, cols]
                v = v_ref[0, rows, cols]
                b_col = cum[:, sl:sl + 1]
                li_col = log_inter[:, sl:sl + 1]
                log_d = jnp.where(mask, b_col + rt[sl:sl + 1, :], -jnp.inf)
                m_t = jnp.maximum(li_col, jnp.max(log_d, axis=-1, keepdims=True))
                w_intra = jnp.exp(log_d - m_t)
                w_prev = jnp.exp(li_col - m_t)
                a = w_intra * _dot_nt(q, k)
                c_old = c_s[sl]
                n_old = n_s[sl:sl + 1, :]
                num = _dot(a.astype(BF), v) + w_prev * _dot_nt(q, c_old.astype(BF))
                den = (jnp.sum(a, axis=-1, keepdims=True)
                       + w_prev * jnp.sum(q.astype(F32) * n_old, axis=-1, keepdims=True))
                denom = jnp.maximum(jnp.abs(den), jnp.exp(-m_t))
                h_ref[0, rows, cols] = (o_ref[0, rows, cols].astype(F32) * (num / denom)).astype(BF)
                w_s = w_s_all[:, sl:sl + 1]
                w_p = w_p_all[:, sl:sl + 1]
                c_s[sl] = w_p * c_old + _dot_tn((w_s * v.astype(F32)).astype(BF), k)
                n_s[sl:sl + 1, :] = w_p * n_old + jnp.sum(w_s * k.astype(F32), axis=0, keepdims=True)
        return carry

    lax.fori_loop(0, nck, chunk_body, 0)

    @pl.when(i == pl.num_programs(1) - 1)
    def _():
        c1_ref[0] = c_s[...]
        n1_ref[0] = n_s[...]
        m1_ref[0] = m_s[...]


def _mlstm_call(q, k, v, o, g, c0, n0, m0):
    b, t, w = q.shape
    tb = _row_tile(t, 512)
    nt = t // tb
    fwd = lambda w_: pl.BlockSpec((1, tb, w_), lambda i, j: (i, j, 0))
    bwd = lambda w_: pl.BlockSpec((1, tb, w_), lambda i, j: (i, nt - 1 - j, 0))
    st_c = pl.BlockSpec((1, N_STREAMS, C_DV, C_DK), lambda i, j: (i, 0, 0, 0))
    st_n = pl.BlockSpec((1, N_STREAMS, C_DK), lambda i, j: (i, 0, 0))
    st_m = pl.BlockSpec((1, 1, LANES), lambda i, j: (i, 0, 0))
    return pl.pallas_call(
        _mlstm_kernel,
        grid=(b, nt),
        in_specs=[fwd(w)] * 4 + [fwd(LANES)] + [bwd(w)] * 4 + [bwd(LANES)] + [st_c, st_n, st_m],
        out_specs=[fwd(w), bwd(w), st_c, st_n, st_m],
        out_shape=[jax.ShapeDtypeStruct((b, t, w), BF), jax.ShapeDtypeStruct((b, t, w), BF),
                   jax.ShapeDtypeStruct(c0.shape, F32), jax.ShapeDtypeStruct(n0.shape, F32),
                   jax.ShapeDtypeStruct(m0.shape, F32)],
        scratch_shapes=[pltpu.VMEM((N_STREAMS, C_DV, C_DK), F32), pltpu.VMEM((N_STREAMS, C_DK), F32),
                        pltpu.VMEM((1, LANES), F32)],
        compiler_params=_params("parallel", "arbitrary"),
        name="mlstm",
    )(q, k, v, o, g, q, k, v, o, g, c0, n0, m0)


def _fourier_kernel(d_ref, dft_t_ref, dft_c_ref, o_ref, y_s, *, scale):
    t = d_ref.shape[1]

    @pl.when(pl.program_id(1) == 0)
    def _():
        step = min(t, 512)
        for r in range(t // step):
            rows = slice(r * step, (r + 1) * step)
            for g in range(D_GROUPS):
                cols = slice(g * D_DIM, (g + 1) * D_DIM)
                y = _dot(d_ref[0, rows, cols], dft_c_ref[...])
                y_s[rows, cols] = y[:, 0:D_DIM].astype(BF)
                y_s[t + r * step:t + (r + 1) * step, cols] = y[:, D_DIM:2 * D_DIM].astype(BF)

    o_ref[0] = (_dot(dft_t_ref[...], y_s[...]) * scale).astype(BF)


def _fourier_call(d, dft_t, dft_c):
    b, t, w = d.shape
    tr = _row_tile(t, 512)
    return pl.pallas_call(
        functools.partial(_fourier_kernel, scale=float((t * D_DIM) ** -0.5)),
        grid=(b, t // tr),
        in_specs=[
            pl.BlockSpec((1, t, w), lambda i, j: (i, 0, 0)),
            pl.BlockSpec((tr, 2 * t), lambda i, j: (j, 0)),
            _const_spec(dft_c.shape),
        ],
        out_specs=pl.BlockSpec((1, tr, w), lambda i, j: (i, j, 0)),
        out_shape=jax.ShapeDtypeStruct((b, t, w), BF),
        scratch_shapes=[pltpu.VMEM((2 * t, w), BF)],
        compiler_params=_params("parallel", "arbitrary"),
        name="fourier",
    )(d, dft_t, dft_c)


def _dft_cos_sin(n):
    two_pi = 2.0 * math.pi
    k = jnp.arange(n, dtype=jnp.int32)
    if n <= 1024:
        ang = ((k[:, None] * k[None, :]) % n).astype(F32) * (two_pi / n)
        return jnp.cos(ang), jnp.sin(ang)
    r = 64
    assert n % r == 0 and (n // r) * n < 2 ** 31
    hi = jnp.arange(n // r, dtype=jnp.int32)
    lo = jnp.arange(r, dtype=jnp.int32)
    ang_hi = ((hi[:, None] * r * k[None, :]) % n).astype(F32) * (two_pi / n)
    ang_lo = ((lo[:, None] * k[None, :]) % n).astype(F32) * (two_pi / n)
    ch, sh = jnp.cos(ang_hi)[:, None, :], jnp.sin(ang_hi)[:, None, :]
    cl, sl = jnp.cos(ang_lo)[None, :, :], jnp.sin(ang_lo)[None, :, :]
    return (ch * cl - sh * sl).reshape(n, n), (sh * cl + ch * sl).reshape(n, n)


def _fourier_tables(t):
    ct, st = _dft_cos_sin(t)
    cc, sc = _dft_cos_sin(D_DIM)
    return jnp.concatenate([ct, -st], axis=1).astype(BF), jnp.concatenate([cc, sc], axis=1).astype(BF)


def _rope_tables(t):
    rows = t // GRID_W
    r = jnp.repeat(jnp.arange(rows), GRID_W).astype(F32)
    cidx = jnp.tile(jnp.arange(GRID_W), rows).astype(F32)
    n_freq = HEAD_DIM // 4
    inv = ROPE_THETA ** (-jnp.arange(n_freq, dtype=F32) / n_freq)
    ang = jnp.concatenate([r[:, None] * inv, cidx[:, None] * inv], axis=-1)
    ang = jnp.concatenate([ang] * (2 * LANES // HEAD_DIM), axis=-1)
    lane = jnp.arange(LANES)
    sign = jnp.where((lane % HEAD_DIM) < HEAD_DIM // 2, -1.0, 1.0).astype(F32)
    return jnp.cos(ang), jnp.sin(ang) * sign


def _dup_heads(a):
    parts = []
    for h in range(B_KV):
        blk = a[..., h * HEAD_DIM:(h + 1) * HEAD_DIM]
        parts += [blk, blk]
    return jnp.concatenate(parts, axis=-1)


def kernel(x_prompt, x_sample, c, cache_k, cache_v, state_C, state_n, state_m, c_ctx, norm_gain, w_mod, b_mod,
           ffn_w_in, ffn_w_out, w_in_even, w_out_even, spatial_w, spatial_b, sgu_gain, q_gain, k_gain,
           w_in_odd, b_gate_odd, w_out_odd):
    depth = norm_gain.shape[0]
    batch, seq, d_model = x_prompt.shape
    dec_batch, dec_seq, _ = x_sample.shape

    rows = 1 + dec_batch
    rows_pad = -(-rows // 8) * 8
    c_rows = jnp.concatenate([c_ctx[None, :], c, jnp.zeros((rows_pad - rows, d_model), F32)], axis=0)
    mod_all = _mod_call(c_rows, w_mod, b_mod)
    mod_ctx = mod_all[:, 0:1].reshape(depth, 1, 3 * N_SUB, d_model)
    mod_lat = mod_all[:, 1:rows].reshape(depth, dec_batch, 3 * N_SUB, d_model)

    ffn_in = ffn_w_in.astype(BF)
    ffn_out = ffn_w_out.astype(BF)

    kv = B_KV * HEAD_DIM
    lane = jnp.arange(LANES)
    bd = ((jnp.arange(B_WIDTH)[:, None] // HEAD_DIM) == (jnp.arange(B_WIDTH)[None, :] // HEAD_DIM))
    bd = jnp.where(bd, 1.0 / HEAD_DIM, 0.0).astype(BF)

    def even_weights(e):
        w = w_in_even[e]
        o = 2 * A_WIDTH + B_WIDTH
        w_cat = jnp.concatenate([w[:, :o], _dup_heads(w[:, o:o + kv]), _dup_heads(w[:, o + kv:o + 2 * kv])], axis=1)
        bs = jnp.broadcast_to(spatial_b[e][:, :, None], (A_GROUPS, CHUNK, A_DIM)).astype(F32)
        return dict(
            w=w_cat.astype(BF), sgu=sgu_gain[e].reshape(1, A_WIDTH), ws=spatial_w[e].astype(BF), bs=bs,
            qg=jnp.tile(q_gain[e], B_HEADS)[None, :], kg=jnp.tile(k_gain[e], 2 * B_KV)[None, :],
            wo_a=w_out_even[e][:A_WIDTH].astype(BF), wo_b=w_out_even[e][A_WIDTH:].astype(BF))

    def odd_weights(e):
        w = w_in_odd[e]
        o = 2 * C_HEADS * C_DK + 2 * C_WIDTH
        n_g = 2 * N_STREAMS
        w_main = jnp.concatenate([w[:, :o], w[:, o + n_g:]], axis=1)
        perm = jnp.array([d * 2 * C_HEADS + g * C_HEADS + h
                          for g in range(2) for d in range(N_DIR) for h in range(C_HEADS)])
        wg = jnp.zeros((d_model, LANES), F32).at[:, :n_g].set(w[:, o:o + n_g][:, perm])
        bg = jnp.zeros((1, LANES), F32).at[0, :n_g].set(b_gate_odd[e].reshape(-1)[perm])
        return dict(w=w_main.astype(BF), wg=wg.astype(BF), bg=bg,
                    wo_a=w_out_odd[e][:C_WIDTH].astype(BF), wo_b=w_out_odd[e][C_WIDTH:].astype(BF))

    def run_pass(x, mods, b, t, rope, ctx_even, ctx_odd, is_ctx):
        bm = x.shape[0]
        per_seq = lambda a: a.reshape(b, t, a.shape[-1])
        flat = lambda a: a.reshape(bm, -1, a.shape[-1])
        new_even, new_odd = [], []
        for l in range(depth):
            e = l // 2
            mod, gain = mods[l], norm_gain[l]
            x = _ffn_call(x, mod, gain, ffn_in[l, 0], ffn_out[l, 0], sub=0)
            if l % 2 == 0:
                p = even_weights(e)
                outs = _even_in_call(x, mod, gain, p["w"], p["sgu"], p["ws"], p["bs"], p["qg"], p["kg"], bd,
                                     rope, is_ctx)
                oa, q, kd, vd = outs[:4]
                q, kd, vd = per_seq(q), per_seq(kd), per_seq(vd)
                if is_ctx:
                    new_even.append((outs[4], outs[5]))
                else:
                    ck, cv = ctx_even[e]
                    kd = jnp.concatenate([_dup_heads(ck.reshape(b, -1, kv)).astype(BF), kd], axis=1)
                    vd = jnp.concatenate([_dup_heads(cv.reshape(b, -1, kv)).astype(BF), vd], axis=1)
                ob = _attn_call(q, kd, vd)
                mix = ([oa], flat(ob), p["wo_a"], p["wo_b"])
            else:
                p = odd_weights(e)
                q, k, v, o, dd, g = _odd_in_call(x, mod, gain, p["w"], p["wg"], p["bg"])
                c0, n0, m0 = ctx_odd[e]
                hf, hb, c1, n1, m1 = _mlstm_call(per_seq(q), per_seq(k), per_seq(v), per_seq(o), per_seq(g),
                                                 c0, n0, m0)
                new_odd.append((c1, n1, m1))
                od = _fourier_call(per_seq(dd), *_fourier_tables(t))
                mix = ([flat(hf), flat(hb)], flat(od), p["wo_a"], p["wo_b"])
            x = _ffn_call(x, mod, gain, ffn_in[l, 1], ffn_out[l, 1], sub=2, mix=mix)
        return x, new_even, new_odd

    n_even = (depth + 1) // 2
    n_odd = depth // 2

    zero_state = (jnp.zeros((batch, N_STREAMS, C_DV, C_DK), F32), jnp.zeros((batch, N_STREAMS, C_DK), F32),
                  jnp.zeros((batch, 1, LANES), F32))
    y, new_even, new_odd = run_pass(x_prompt.reshape(1, batch * seq, d_model), list(mod_ctx), batch, seq,
                                    None, None, [zero_state] * n_odd, True)
    y_prompt = y.reshape(batch, seq, d_model)
    new_cache_k = jnp.stack([k.reshape(batch, seq, B_KV, HEAD_DIM) for k, _ in new_even], axis=1)
    new_cache_v = jnp.stack([v.reshape(batch, seq, B_KV, HEAD_DIM) for _, v in new_even], axis=1)
    new_state_c = jnp.stack([c1.reshape(batch, N_DIR, C_HEADS, C_DV, C_DK) for c1, _, _ in new_odd], axis=1)
    new_state_n = jnp.stack([n1.reshape(batch, N_DIR, C_HEADS, C_DK) for _, n1, _ in new_odd], axis=1)
    new_state_m = jnp.stack([m1[:, 0, :N_STREAMS].reshape(batch, N_DIR, C_HEADS) for _, _, m1 in new_odd], axis=1)

    ctx_even = [(cache_k[:, e], cache_v[:, e]) for e in range(n_even)]
    ctx_odd = []
    for e in range(n_odd):
        m0 = jnp.zeros((dec_batch, 1, LANES), F32).at[:, 0, :N_STREAMS].set(state_m[:, e].reshape(dec_batch, -1))
        ctx_odd.append((state_C[:, e].reshape(dec_batch, N_STREAMS, C_DV, C_DK),
                        state_n[:, e].reshape(dec_batch, N_STREAMS, C_DK), m0))
    y_sample, _, _ = run_pass(x_sample, list(mod_lat), dec_batch, dec_seq, _rope_tables(dec_seq),
                              ctx_even, ctx_odd, False)

    return (y_prompt, y_sample, new_cache_k, new_cache_v, new_state_c, new_state_n, new_state_m)
```

```python
import functools
import math

import jax
import jax.numpy as jnp
from jax import lax
from jax.experimental import pallas as pl
from jax.experimental.pallas import tpu as pltpu

BF = jnp.bfloat16
F32 = jnp.float32

EPS = 1e-6
ROPE_THETA = 10000.0
GRID_W = 64
CHUNK = 128
N_SUB = 3
A_GROUPS = 4
A_DIM = 128
A_WIDTH = A_GROUPS * A_DIM
B_HEADS = 8
B_KV = 2
HEAD_DIM = 64
B_WIDTH = B_HEADS * HEAD_DIM
C_HEADS = 4
C_DK = 128
C_DV = 128
C_WIDTH = C_HEADS * C_DV
N_DIR = 2
N_STREAMS = N_DIR * C_HEADS
D_GROUPS = 4
D_DIM = 128
D_WIDTH = D_GROUPS * D_DIM
LANES = 128

VMEM_LIMIT = 56 * 1024 * 1024


def _params(*sem):
    return pltpu.CompilerParams(dimension_semantics=sem, vmem_limit_bytes=VMEM_LIMIT)


def _dot(a, b):
    return jnp.dot(a, b, preferred_element_type=F32)


def _dot_nt(a, b):
    return lax.dot_general(a, b, (((1,), (1,)), ((), ())), preferred_element_type=F32)


def _dot_tn(a, b):
    return lax.dot_general(a, b, (((0,), (0,)), ((), ())), preferred_element_type=F32)


def _sigmoid(x):
    return 1.0 / (1.0 + jnp.exp(-x))


def _gelu_tanh(x):
    return 0.5 * x * (1.0 + jnp.tanh(math.sqrt(2.0 / math.pi) * (x + 0.044715 * (x * x * x))))


def _norm_mod(x, gain, shift, scale):
    ms = jnp.mean(x * x, axis=-1, keepdims=True)
    y = x * lax.rsqrt(ms + EPS) * gain
    return y * (1.0 + scale) + shift


def _const_spec(shape):
    zeros = (0,) * len(shape)
    return pl.BlockSpec(shape, lambda *_: zeros, pipeline_mode=pl.Buffered(1))


def _row_tile(t, target):
    tm = min(t, target)
    assert t % tm == 0
    return tm


def _mod_kernel(c_ref, w_ref, b_ref, o_ref):
    c = c_ref[...]
    sc = (c * _sigmoid(c)).astype(BF)
    o_ref[0] = _dot(sc, w_ref[0].astype(BF)) + b_ref[0]


def _mod_call(c_rows, w_mod, b_mod):
    depth, d, n = w_mod.shape
    rows = c_rows.shape[0]
    tn = 1024
    assert n % tn == 0
    return pl.pallas_call(
        _mod_kernel,
        grid=(depth, n // tn),
        in_specs=[
            pl.BlockSpec((rows, d), lambda l, j: (0, 0)),
            pl.BlockSpec((1, d, tn), lambda l, j: (l, 0, j)),
            pl.BlockSpec((1, 1, tn), lambda l, j: (l, 0, j)),
        ],
        out_specs=pl.BlockSpec((1, rows, tn), lambda l, j: (l, 0, j)),
        out_shape=jax.ShapeDtypeStruct((depth, rows, n), F32),
        compiler_params=_params("parallel", "parallel"),
        name="mod",
    )(c_rows, w_mod, b_mod.reshape(depth, 1, n))


FFN_CHUNK = 256


def _ffn_kernel(*refs, n_mix_a, sub):
    if n_mix_a:
        x_ref = refs[0]
        a_refs = refs[1:1 + n_mix_a]
        b_ref, mod_ref, gain_ref, wa_ref, wb_ref, win_ref, wout_ref, o_ref, act_ref = refs[1 + n_mix_a:]
    else:
        x_ref, mod_ref, gain_ref, win_ref, wout_ref, o_ref, act_ref = refs
    x = x_ref[0]
    if n_mix_a:
        if n_mix_a == 1:
            a = a_refs[0][0]
        else:
            a = (a_refs[0][0].astype(F32) + a_refs[1][0].astype(F32)).astype(BF)
        mix = _dot(a, wa_ref[...]) + _dot(b_ref[0], wb_ref[...])
        x = x + mod_ref[0, 5:6, :] * mix
    h = _norm_mod(x, gain_ref[sub:sub + 1, :], mod_ref[0, 3 * sub:3 * sub + 1, :],
                  mod_ref[0, 3 * sub + 1:3 * sub + 2, :]).astype(BF)
    d_ff = act_ref.shape[1]
    for c in range(d_ff // FFN_CHUNK):
        lo = c * FFN_CHUNK
        g = _dot(h, win_ref[:, lo:lo + FFN_CHUNK])
        u = _dot(h, win_ref[:, d_ff + lo:d_ff + lo + FFN_CHUNK])
        act_ref[:, lo:lo + FFN_CHUNK] = (g * _sigmoid(g) * u).astype(BF)
    y = _dot(act_ref[...], wout_ref[...])
    o_ref[0] = x + (0.5 * mod_ref[0, 3 * sub + 2:3 * sub + 3, :]) * y


def _ffn_call(x, mod, gain, w_in, w_out, sub, mix=None):
    bm, t, d = x.shape
    d_ff = w_out.shape[0]
    tm = _row_tile(t, 512)
    row = lambda w: pl.BlockSpec((1, tm, w), lambda b, i: (b, i, 0))
    in_specs = [row(d)]
    args = [x]
    n_mix_a = 0
    if mix is not None:
        a_list, b_arr, w_a, w_b = mix
        n_mix_a = len(a_list)
        for a in a_list:
            in_specs.append(row(a.shape[-1]))
            args.append(a)
        in_specs.append(row(b_arr.shape[-1]))
        args.append(b_arr)
    in_specs += [pl.BlockSpec((1, 3 * N_SUB, d), lambda b, i: (b, 0, 0)), _const_spec(gain.shape)]
    args += [mod, gain]
    if mix is not None:
        in_specs += [_const_spec(w_a.shape), _const_spec(w_b.shape)]
        args += [w_a, w_b]
    in_specs += [_const_spec(w_in.shape), _const_spec(w_out.shape)]
    args += [w_in, w_out]
    return pl.pallas_call(
        functools.partial(_ffn_kernel, n_mix_a=n_mix_a, sub=sub),
        grid=(bm, t // tm),
        in_specs=in_specs,
        out_specs=row(d),
        out_shape=jax.ShapeDtypeStruct((bm, t, d), F32),
        scratch_shapes=[pltpu.VMEM((tm, d_ff), BF)],
        compiler_params=_params("parallel", "parallel"),
        name="ffn_mix" if mix is not None else "ffn",
    )(*args)


def _rope(x, cos, sin_signed):
    w = x.shape[1]
    n = w // LANES
    cos = jnp.concatenate([cos] * n, axis=1)
    sin_signed = jnp.concatenate([sin_signed] * n, axis=1)
    lane = lax.broadcasted_iota(jnp.int32, (1, w), 1)
    first_half = (lane & (HEAD_DIM - 1)) < (HEAD_DIM // 2)
    fwd = pltpu.roll(x, w - HEAD_DIM // 2, axis=1)
    bwd = pltpu.roll(x, HEAD_DIM // 2, axis=1)
    return x * cos + jnp.where(first_half, fwd, bwd) * sin_signed


def _even_in_kernel(*refs, has_rope, is_ctx):
    it = iter(refs)
    x_ref, mod_ref, gain_ref, w_ref, sgu_ref, ws_ref, bs_ref, qg_ref, kg_ref, bd_ref = (next(it) for _ in range(10))
    if has_rope:
        cos_ref, sin_ref = next(it), next(it)
    oa_ref, q_ref, kd_ref, vd_ref = (next(it) for _ in range(4))
    if is_ctx:
        k32_ref, v32_ref = next(it), next(it)

    tm = x_ref.shape[1]
    nck = tm // CHUNK
    x = x_ref[0]
    h = _norm_mod(x, gain_ref[1:2, :], mod_ref[0, 3:4, :], mod_ref[0, 4:5, :]).astype(BF)

    au = _gelu_tanh(_dot(h, w_ref[:, 0:A_WIDTH]))
    av = _gelu_tanh(_dot(h, w_ref[:, A_WIDTH:2 * A_WIDTH]))
    for g in range(A_GROUPS):
        cols = slice(g * A_DIM, (g + 1) * A_DIM)
        blk = av[:, cols]
        ms = jnp.mean(blk * blk, axis=-1, keepdims=True)
        vg = (blk * lax.rsqrt(ms + EPS) * sgu_ref[:, cols]).astype(BF)
        rhs = jnp.concatenate([vg[c * CHUNK:(c + 1) * CHUNK, :] for c in range(nck)], axis=1)
        s = _dot(ws_ref[g], rhs)
        for c in range(nck):
            rows = slice(c * CHUNK, (c + 1) * CHUNK)
            gate = s[:, c * A_DIM:(c + 1) * A_DIM] + bs_ref[g]
            oa_ref[0, rows, cols] = (au[rows, cols] * gate).astype(BF)

    lane = lax.broadcasted_iota(jnp.int32, (1, LANES), 1)
    lo_half = lane < HEAD_DIM
    if has_rope:
        cos = cos_ref[...]
        sin = sin_ref[...]

    off = 2 * A_WIDTH
    q = _dot(h, w_ref[:, off:off + B_WIDTH])
    ms = _dot((q * q).astype(BF), bd_ref[...])
    q = q * lax.rsqrt(ms + EPS) * qg_ref[...]
    if has_rope:
        q = _rope(q, cos, sin)
    q_ref[0] = (q * (HEAD_DIM ** -0.5)).astype(BF)

    off += B_WIDTH
    kvw = 2 * B_KV * HEAD_DIM
    k = _dot(h, w_ref[:, off:off + kvw])
    ms = _dot((k * k).astype(BF), bd_ref[0:kvw, 0:kvw])
    k = k * lax.rsqrt(ms + EPS) * kg_ref[...]
    if is_ctx:
        k32_ref[0] = jnp.where(lo_half, k[:, 0:LANES], k[:, LANES:2 * LANES])
    if has_rope:
        k = _rope(k, cos, sin)
    kd_ref[0] = k.astype(BF)
    off += kvw
    v = _dot(h, w_ref[:, off:off + kvw])
    if is_ctx:
        v32_ref[0] = jnp.where(lo_half, v[:, 0:LANES], v[:, LANES:2 * LANES])
    vd_ref[0] = v.astype(BF)


def _even_in_call(x, mod, gain, w, sgu, ws, bs, qg, kg, bd, rope, is_ctx):
    bm, t, d = x.shape
    tm = _row_tile(t, 512)
    kvw = 2 * B_KV * HEAD_DIM
    row = lambda w_: pl.BlockSpec((1, tm, w_), lambda b, i: (b, i, 0))
    in_specs = [row(d), pl.BlockSpec((1, 3 * N_SUB, d), lambda b, i: (b, 0, 0))]
    in_specs += [_const_spec(a.shape) for a in (gain, w, sgu, ws, bs, qg, kg, bd)]
    args = [x, mod, gain, w, sgu, ws, bs, qg, kg, bd]
    if rope is not None:
        in_specs += [pl.BlockSpec((tm, LANES), lambda b, i: (i, 0))] * 2
        args += list(rope)
    out_specs = [row(A_WIDTH), row(B_WIDTH), row(kvw), row(kvw)]
    out_shape = [jax.ShapeDtypeStruct((bm, t, A_WIDTH), BF), jax.ShapeDtypeStruct((bm, t, B_WIDTH), BF),
                 jax.ShapeDtypeStruct((bm, t, kvw), BF), jax.ShapeDtypeStruct((bm, t, kvw), BF)]
    if is_ctx:
        out_specs += [row(B_KV * HEAD_DIM)] * 2
        out_shape += [jax.ShapeDtypeStruct((bm, t, B_KV * HEAD_DIM), F32)] * 2
    return pl.pallas_call(
        functools.partial(_even_in_kernel, has_rope=rope is not None, is_ctx=is_ctx),
        grid=(bm, t // tm),
        in_specs=in_specs,
        out_specs=out_specs,
        out_shape=out_shape,
        compiler_params=_params("parallel", "parallel"),
        name="even_in",
    )(*args)


def _attn_kernel(q_ref, kd_ref, vd_ref, o_ref):
    lane = lax.broadcasted_iota(jnp.int32, (1, LANES), 1)
    lo_half = lane < HEAD_DIM
    heads_per_kv = B_HEADS // B_KV
    for p in range(B_WIDTH // LANES):
        cols = slice(p * LANES, (p + 1) * LANES)
        kv = (2 * p) // heads_per_kv
        kcols = slice(kv * LANES, (kv + 1) * LANES)
        qp = q_ref[0, :, cols]
        k2 = kd_ref[0, :, kcols]
        v2 = vd_ref[0, :, kcols]
        outs = []
        for half in range(2):
            keep = lo_half if half == 0 else jnp.logical_not(lo_half)
            qm = jnp.where(keep, qp, jnp.zeros_like(qp))
            s = _dot_nt(qm, k2)
            m = jnp.max(s, axis=-1, keepdims=True)
            e = jnp.exp(s - m)
            l = jnp.sum(e, axis=-1, keepdims=True)
            outs.append(_dot(e.astype(BF), v2) / l)
        o_ref[0, :, cols] = jnp.where(lo_half, outs[0], outs[1]).astype(BF)


def _attn_call(q, kd, vd):
    b, t, w = q.shape
    s = kd.shape[1]
    tq = _row_tile(t, 256)
    return pl.pallas_call(
        _attn_kernel,
        grid=(b, t // tq),
        in_specs=[
            pl.BlockSpec((1, tq, w), lambda i, j: (i, j, 0)),
            pl.BlockSpec((1, s, kd.shape[2]), lambda i, j: (i, 0, 0)),
            pl.BlockSpec((1, s, vd.shape[2]), lambda i, j: (i, 0, 0)),
        ],
        out_specs=pl.BlockSpec((1, tq, w), lambda i, j: (i, j, 0)),
        out_shape=jax.ShapeDtypeStruct((b, t, w), BF),
        compiler_params=_params("parallel", "parallel"),
        name="attn",
    )(q, kd, vd)


N_GATE_ROWS = 2 * N_STREAMS


def _log_sigmoid(x):
    return jnp.minimum(x, 0.0) - jnp.log(1.0 + jnp.exp(-jnp.abs(x)))


def _odd_in_kernel(x_ref, mod_ref, gain_ref, w_ref, wt_ref, wgt_ref, bg_ref,
                   q_ref, k_ref, d_ref, vt_ref, ot_ref, gt_ref):
    tm = x_ref.shape[1]
    x = x_ref[0]
    h = _norm_mod(x, gain_ref[1:2, :], mod_ref[0, 3:4, :], mod_ref[0, 4:5, :]).astype(BF)
    w = C_WIDTH
    q_ref[0] = _dot(h, w_ref[:, 0:w]).astype(BF)
    k_ref[0] = (_dot(h, w_ref[:, w:2 * w]) * (C_DK ** -0.5)).astype(BF)
    d_ref[0] = _dot(h, w_ref[:, 2 * w:2 * w + D_WIDTH]).astype(BF)
    vt = _dot_nt(wt_ref[0:w, :], h)
    ot = _sigmoid(_dot_nt(wt_ref[w:2 * w, :], h))
    gt = _dot_nt(wgt_ref[...], h)
    row = lax.broadcasted_iota(jnp.int32, (N_GATE_ROWS, CHUNK), 0)
    for c in range(tm // CHUNK):
        lanes = slice(c * CHUNK, (c + 1) * CHUNK)
        vt_ref[0, c] = vt[:, lanes].astype(BF)
        ot_ref[0, c] = ot[:, lanes].astype(BF)
        g = gt[:, lanes] + bg_ref[...]
        gt_ref[0, c] = jnp.where(row >= N_STREAMS, _log_sigmoid(g), g)


def _odd_in_call(x, mod, gain, w, wt, wgt, bg):
    bm, t, d = x.shape
    tm = _row_tile(t, 512)
    nck = tm // CHUNK
    row = lambda w_: pl.BlockSpec((1, tm, w_), lambda b, i: (b, i, 0))
    chunked = lambda r: pl.BlockSpec((1, nck, r, CHUNK), lambda b, i: (b, i, 0, 0))
    return pl.pallas_call(
        _odd_in_kernel,
        grid=(bm, t // tm),
        in_specs=[row(d), pl.BlockSpec((1, 3 * N_SUB, d), lambda b, i: (b, 0, 0))]
        + [_const_spec(a.shape) for a in (gain, w, wt, wgt, bg)],
        out_specs=[row(C_WIDTH), row(C_WIDTH), row(D_WIDTH), chunked(C_WIDTH), chunked(C_WIDTH),
                   chunked(N_GATE_ROWS)],
        out_shape=[jax.ShapeDtypeStruct((bm, t, C_WIDTH), BF), jax.ShapeDtypeStruct((bm, t, C_WIDTH), BF),
                   jax.ShapeDtypeStruct((bm, t, D_WIDTH), BF),
                   jax.ShapeDtypeStruct((bm, t // CHUNK, C_WIDTH, CHUNK), BF),
                   jax.ShapeDtypeStruct((bm, t // CHUNK, C_WIDTH, CHUNK), BF),
                   jax.ShapeDtypeStruct((bm, t // CHUNK, N_GATE_ROWS, CHUNK), F32)],
        compiler_params=_params("parallel", "parallel"),
        name="odd_in",
    )(x, mod, gain, w, wt, wgt, bg)


N_REP = 8
CN_ROWS = C_DV + N_REP


def _split3(x):
    hi = x.astype(BF)
    r1 = x - hi.astype(F32)
    mid = r1.astype(BF)
    lo = (r1 - mid.astype(F32)).astype(BF)
    return hi, mid, lo


def _mlstm_kernel(qf_ref, kf_ref, vtf_ref, otf_ref, gtf_ref, qb_ref, kb_ref, vtb_ref, otb_ref, gtb_ref,
                  c0_ref, n0_ref, m0_ref,
                  hf_ref, hb_ref, c1_ref, n1_ref, m1_ref,
                  cn_s, m_s):
    i = pl.program_id(1)
    nck = qf_ref.shape[1] // CHUNK

    @pl.when(i == 0)
    def _():
        for sl in range(N_STREAMS):
            cn_s[sl, 0:C_DV, :] = c0_ref[0, sl]
            cn_s[sl, C_DV:CN_ROWS, :] = jnp.broadcast_to(n0_ref[0, sl:sl + 1, :], (N_REP, C_DK))
        m_s[...] = m0_ref[0]

    r_i = lax.broadcasted_iota(jnp.int32, (CHUNK, CHUNK), 0)
    c_i = lax.broadcasted_iota(jnp.int32, (CHUNK, CHUNK), 1)
    upper = c_i >= r_i
    lower = c_i <= r_i
    tri_up = jnp.where(upper, 1.0, 0.0).astype(BF)
    tri_lo = jnp.where(lower, 1.0, 0.0).astype(BF)
    eye = jnp.where(c_i == r_i, 1.0, 0.0).astype(BF)
    ones = jnp.ones((CHUNK, CHUNK), BF)
    n_rows = nck * N_STREAMS
    assert n_rows <= CHUNK
    is_fwd = (lax.broadcasted_iota(jnp.int32, (n_rows, CHUNK), 0) & (N_STREAMS - 1)) < C_HEADS
    is_fwd8 = is_fwd[0:N_STREAMS]
    lane = lax.broadcasted_iota(jnp.int32, (n_rows, CHUNK), 1)

    g_i, g_f = [], []
    for c in range(nck):
        gf = gtf_ref[0, c]
        gb = gtb_ref[0, nck - 1 - c]
        g_i.append(jnp.where(is_fwd8, gf[0:N_STREAMS], gb[0:N_STREAMS]))
        g_f.append(jnp.where(is_fwd8, gf[N_STREAMS:N_GATE_ROWS], gb[N_STREAMS:N_GATE_ROWS]))
    g_i = jnp.concatenate(g_i, axis=0)
    g_f = jnp.concatenate(g_f, axis=0)
    parts = _split3(g_f)
    cum = jnp.where(is_fwd, sum(_dot(p, tri_up) for p in parts),
                    sum(_dot(p, tri_lo) for p in parts))
    total = sum(_dot(p, ones) for p in parts)
    r = g_i - cum
    run = r
    sh = 1
    while sh < CHUNK:
        prev = jnp.where(lane >= sh, pltpu.roll(run, sh, axis=1), -jnp.inf)
        nxt = jnp.where(lane < CHUNK - sh, pltpu.roll(run, CHUNK - sh, axis=1), -jnp.inf)
        run = jnp.maximum(run, jnp.where(is_fwd, prev, nxt))
        sh *= 2
    log_s = total - cum + g_i
    ls_max = jnp.max(log_s, axis=-1, keepdims=True)
    m = m_s[...]
    m_old, m_new = [], []
    for c in range(nck):
        rows = slice(c * N_STREAMS, (c + 1) * N_STREAMS)
        m_old.append(m)
        m = jnp.maximum(total[rows] + m, ls_max[rows])
        m_new.append(m)
    m_s[...] = m
    m_old = jnp.concatenate(m_old, axis=0)
    m_new = jnp.concatenate(m_new, axis=0)
    log_inter = cum + m_old
    m_t = jnp.maximum(log_inter, cum + run)
    bm = cum - m_t
    w_prev = jnp.exp(log_inter - m_t)
    e_inv = jnp.exp(-m_t)
    w_src = jnp.exp(log_s - m_new)
    w_keep = jnp.exp(total + m_old - m_new)
    r_cols = jnp.concatenate([r, jnp.zeros((CHUNK - n_rows, CHUNK), F32)], axis=0).T

    row = lambda a, j: a[j:j + 1, :]
    units = []
    for c in range(nck):
        for sl in range(N_STREAMS):
            fwd = sl < C_HEADS
            q_ref, k_ref, vt_ref, ot_ref, h_ref = ((qf_ref, kf_ref, vtf_ref, otf_ref, hf_ref) if fwd
                                                   else (qb_ref, kb_ref, vtb_ref, otb_ref, hb_ref))
            ck = c if fwd else nck - 1 - c
            rows = slice(ck * CHUNK, (ck + 1) * CHUNK)
            cols = slice((sl % C_HEADS) * C_DK, (sl % C_HEADS + 1) * C_DK)
            units.append((c * N_STREAMS + sl, sl, q_ref[0, rows, cols], k_ref[0, rows, cols],
                          vt_ref[0, ck, cols, :], ot_ref[0, ck, cols, :], h_ref, rows, cols))
    s_t = [_dot_nt(k, q) for _, _, q, k, _, _, _, _, _ in units]
    upd = []
    for j, _, _, k, vt, _, _, _, _ in units:
        ws = row(w_src, j)
        lhs = jnp.concatenate([(vt.astype(F32) * ws).astype(BF),
                               jnp.broadcast_to(ws, (N_REP, CHUNK)).astype(BF)], axis=0)
        upd.append(_dot(lhs, k))
    a_t = []
    for (j, sl, *_), s in zip(units, s_t):
        mask = upper if sl < C_HEADS else lower
        a_t.append(jnp.exp(jnp.where(mask, r_cols[:, j:j + 1] + row(bm, j), -jnp.inf)) * s)
    num = [_dot(u[4], a.astype(BF)) for u, a in zip(units, a_t)]
    cn = [cn_s[sl] for sl in range(N_STREAMS)]
    step_units = lambda c: range(c * N_STREAMS, (c + 1) * N_STREAMS)

    def read_state(c):
        out = []
        for idx in step_units(c):
            j, sl, q = units[idx][0:3]
            out.append(_dot_nt(cn[sl].astype(BF), q))
            cn[sl] = row(w_keep, j) * cn[sl] + upd[idx]
        return out

    inter = read_state(0)
    for c in range(nck):
        h_t = []
        for idx, it in zip(step_units(c), inter):
            j, ot = units[idx][0], units[idx][5]
            wp = row(w_prev, j)
            den = jnp.sum(a_t[idx], axis=0, keepdims=True) + wp * it[C_DV:C_DV + 1]
            denom = jnp.maximum(jnp.abs(den), row(e_inv, j))
            h_t.append(((num[idx] + wp * it[0:C_DV]) * (1.0 / denom) * ot.astype(F32)).astype(BF))
        if c + 1 < nck:
            inter = read_state(c + 1)
        h_rows = [_dot_nt(eye, h).astype(BF) for h in h_t]
        for idx, h in zip(step_units(c), h_rows):
            h_ref, rows, cols = units[idx][6:9]
            h_ref[0, rows, cols] = h
    for sl in range(N_STREAMS):
        cn_s[sl] = cn[sl]

    @pl.when(i == pl.num_programs(1) - 1)
    def _():
        for sl in range(N_STREAMS):
            c1_ref[0, sl] = cn_s[sl, 0:C_DV, :]
            n1_ref[0, sl:sl + 1, :] = cn_s[sl, C_DV:C_DV + 1, :]
        m1_ref[0] = m_s[...]


def _mlstm_call(q, k, vt, ot, gt, c0, n0, m0):
    b, t, w = q.shape
    tb = _row_tile(t, 512)
    nt = t // tb
    nck = tb // CHUNK
    fwd = pl.BlockSpec((1, tb, w), lambda i, j: (i, j, 0))
    bwd = pl.BlockSpec((1, tb, w), lambda i, j: (i, nt - 1 - j, 0))
    fwd_t = lambda r: pl.BlockSpec((1, nck, r, CHUNK), lambda i, j: (i, j, 0, 0))
    bwd_t = lambda r: pl.BlockSpec((1, nck, r, CHUNK), lambda i, j: (i, nt - 1 - j, 0, 0))
    st_c = pl.BlockSpec((1, N_STREAMS, C_DV, C_DK), lambda i, j: (i, 0, 0, 0))
    st_n = pl.BlockSpec((1, N_STREAMS, C_DK), lambda i, j: (i, 0, 0))
    st_m = pl.BlockSpec((1, N_STREAMS, CHUNK), lambda i, j: (i, 0, 0))
    return pl.pallas_call(
        _mlstm_kernel,
        grid=(b, nt),
        in_specs=[fwd, fwd, fwd_t(w), fwd_t(w), fwd_t(N_GATE_ROWS), bwd, bwd, bwd_t(w), bwd_t(w),
                  bwd_t(N_GATE_ROWS), st_c, st_n, st_m],
        out_specs=[fwd, bwd, st_c, st_n, st_m],
        out_shape=[jax.ShapeDtypeStruct((b, t, w), BF), jax.ShapeDtypeStruct((b, t, w), BF),
                   jax.ShapeDtypeStruct(c0.shape, F32), jax.ShapeDtypeStruct(n0.shape, F32),
                   jax.ShapeDtypeStruct(m0.shape, F32)],
        scratch_shapes=[pltpu.VMEM((N_STREAMS, CN_ROWS, C_DK), F32), pltpu.VMEM((N_STREAMS, CHUNK), F32)],
        compiler_params=_params("parallel", "arbitrary"),
        name="mlstm",
    )(q, k, vt, ot, gt, q, k, vt, ot, gt, c0, n0, m0)


def _fourier_kernel(d_ref, dft_t_ref, dft_c_ref, o_ref, y_s, *, scale):
    t = d_ref.shape[1]

    @pl.when(pl.program_id(1) == 0)
    def _():
        step = min(t, 512)
        for r in range(t // step):
            rows = slice(r * step, (r + 1) * step)
            for g in range(D_GROUPS):
                cols = slice(g * D_DIM, (g + 1) * D_DIM)
                y = _dot(d_ref[0, rows, cols], dft_c_ref[...])
                y_s[rows, cols] = y[:, 0:D_DIM].astype(BF)
                y_s[t + r * step:t + (r + 1) * step, cols] = y[:, D_DIM:2 * D_DIM].astype(BF)

    o_ref[0] = (_dot(dft_t_ref[...], y_s[...]) * scale).astype(BF)


def _fourier_call(d, dft_t, dft_c):
    b, t, w = d.shape
    tr = _row_tile(t, 512)
    return pl.pallas_call(
        functools.partial(_fourier_kernel, scale=float((t * D_DIM) ** -0.5)),
        grid=(b, t // tr),
        in_specs=[
            pl.BlockSpec((1, t, w), lambda i, j: (i, 0, 0)),
            pl.BlockSpec((tr, 2 * t), lambda i, j: (j, 0)),
            _const_spec(dft_c.shape),
        ],
        out_specs=pl.BlockSpec((1, tr, w), lambda i, j: (i, j, 0)),
        out_shape=jax.ShapeDtypeStruct((b, t, w), BF),
        scratch_shapes=[pltpu.VMEM((2 * t, w), BF)],
        compiler_params=_params("parallel", "arbitrary"),
        name="fourier",
    )(d, dft_t, dft_c)


def _dft_cos_sin(n):
    two_pi = 2.0 * math.pi
    k = jnp.arange(n, dtype=jnp.int32)
    if n <= 1024:
        ang = ((k[:, None] * k[None, :]) % n).astype(F32) * (two_pi / n)
        return jnp.cos(ang), jnp.sin(ang)
    r = 64
    assert n % r == 0 and (n // r) * n < 2 ** 31
    hi = jnp.arange(n // r, dtype=jnp.int32)
    lo = jnp.arange(r, dtype=jnp.int32)
    ang_hi = ((hi[:, None] * r * k[None, :]) % n).astype(F32) * (two_pi / n)
    ang_lo = ((lo[:, None] * k[None, :]) % n).astype(F32) * (two_pi / n)
    ch, sh = jnp.cos(ang_hi)[:, None, :], jnp.sin(ang_hi)[:, None, :]
    cl, sl = jnp.cos(ang_lo)[None, :, :], jnp.sin(ang_lo)[None, :, :]
    return (ch * cl - sh * sl).reshape(n, n), (sh * cl + ch * sl).reshape(n, n)


def _fourier_tables(t):
    ct, st = _dft_cos_sin(t)
    cc, sc = _dft_cos_sin(D_DIM)
    return jnp.concatenate([ct, -st], axis=1).astype(BF), jnp.concatenate([cc, sc], axis=1).astype(BF)


def _rope_tables(t):
    rows = t // GRID_W
    r = jnp.repeat(jnp.arange(rows), GRID_W).astype(F32)
    cidx = jnp.tile(jnp.arange(GRID_W), rows).astype(F32)
    n_freq = HEAD_DIM // 4
    inv = ROPE_THETA ** (-jnp.arange(n_freq, dtype=F32) / n_freq)
    ang = jnp.concatenate([r[:, None] * inv, cidx[:, None] * inv], axis=-1)
    ang = jnp.concatenate([ang] * (2 * LANES // HEAD_DIM), axis=-1)
    lane = jnp.arange(LANES)
    sign = jnp.where((lane % HEAD_DIM) < HEAD_DIM // 2, -1.0, 1.0).astype(F32)
    return jnp.cos(ang), jnp.sin(ang) * sign


def _dup_heads(a):
    parts = []
    for h in range(B_KV):
        blk = a[..., h * HEAD_DIM:(h + 1) * HEAD_DIM]
        parts += [blk, blk]
    return jnp.concatenate(parts, axis=-1)


def kernel(x_prompt, x_sample, c, cache_k, cache_v, state_C, state_n, state_m, c_ctx, norm_gain, w_mod, b_mod,
           ffn_w_in, ffn_w_out, w_in_even, w_out_even, spatial_w, spatial_b, sgu_gain, q_gain, k_gain,
           w_in_odd, b_gate_odd, w_out_odd):
    depth = norm_gain.shape[0]
    batch, seq, d_model = x_prompt.shape
    dec_batch, dec_seq, _ = x_sample.shape

    rows = 1 + dec_batch
    rows_pad = -(-rows // 8) * 8
    c_rows = jnp.concatenate([c_ctx[None, :], c, jnp.zeros((rows_pad - rows, d_model), F32)], axis=0)
    mod_all = _mod_call(c_rows, w_mod, b_mod)
    mod_ctx = mod_all[:, 0:1].reshape(depth, 1, 3 * N_SUB, d_model)
    mod_lat = mod_all[:, 1:rows].reshape(depth, dec_batch, 3 * N_SUB, d_model)

    ffn_in = ffn_w_in.astype(BF)
    ffn_out = ffn_w_out.astype(BF)

    kv = B_KV * HEAD_DIM
    lane = jnp.arange(LANES)
    bd = ((jnp.arange(B_WIDTH)[:, None] // HEAD_DIM) == (jnp.arange(B_WIDTH)[None, :] // HEAD_DIM))
    bd = jnp.where(bd, 1.0 / HEAD_DIM, 0.0).astype(BF)

    def even_weights(e):
        w = w_in_even[e]
        o = 2 * A_WIDTH + B_WIDTH
        w_cat = jnp.concatenate([w[:, :o], _dup_heads(w[:, o:o + kv]), _dup_heads(w[:, o + kv:o + 2 * kv])], axis=1)
        bs = jnp.broadcast_to(spatial_b[e][:, :, None], (A_GROUPS, CHUNK, A_DIM)).astype(F32)
        return dict(
            w=w_cat.astype(BF), sgu=sgu_gain[e].reshape(1, A_WIDTH), ws=spatial_w[e].astype(BF), bs=bs,
            qg=jnp.tile(q_gain[e], B_HEADS)[None, :], kg=jnp.tile(k_gain[e], 2 * B_KV)[None, :],
            wo_a=w_out_even[e][:A_WIDTH].astype(BF), wo_b=w_out_even[e][A_WIDTH:].astype(BF))

    def odd_weights(e):
        w = w_in_odd[e]
        o = 2 * C_HEADS * C_DK + 2 * C_WIDTH
        n_g = 2 * N_STREAMS
        qk = 2 * C_HEADS * C_DK
        w_rows = jnp.concatenate([w[:, :qk], w[:, o + n_g:]], axis=1)
        w_t = w[:, qk:o].T
        perm = jnp.array([d * 2 * C_HEADS + g * C_HEADS + h
                          for g in range(2) for d in range(N_DIR) for h in range(C_HEADS)])
        wg_t = w[:, o:o + n_g][:, perm].T
        bg = jnp.broadcast_to(b_gate_odd[e].reshape(-1)[perm][:, None], (n_g, CHUNK)).astype(F32)
        return dict(w=w_rows.astype(BF), wt=w_t.astype(BF), wgt=wg_t.astype(BF), bg=bg,
                    wo_a=w_out_odd[e][:C_WIDTH].astype(BF), wo_b=w_out_odd[e][C_WIDTH:].astype(BF))

    def run_pass(x, mods, b, t, rope, ctx_even, ctx_odd, is_ctx):
        bm = x.shape[0]
        per_seq = lambda a: a.reshape(b, t, a.shape[-1])
        flat = lambda a: a.reshape(bm, -1, a.shape[-1])
        new_even, new_odd = [], []
        for l in range(depth):
            e = l // 2
            mod, gain = mods[l], norm_gain[l]
            x = _ffn_call(x, mod, gain, ffn_in[l, 0], ffn_out[l, 0], sub=0)
            if l % 2 == 0:
                p = even_weights(e)
                outs = _even_in_call(x, mod, gain, p["w"], p["sgu"], p["ws"], p["bs"], p["qg"], p["kg"], bd,
                                     rope, is_ctx)
                oa, q, kd, vd = outs[:4]
                q, kd, vd = per_seq(q), per_seq(kd), per_seq(vd)
                if is_ctx:
                    new_even.append((outs[4], outs[5]))
                else:
                    ck, cv = ctx_even[e]
                    kd = jnp.concatenate([_dup_heads(ck.reshape(b, -1, kv)).astype(BF), kd], axis=1)
                    vd = jnp.concatenate([_dup_heads(cv.reshape(b, -1, kv)).astype(BF), vd], axis=1)
                ob = _attn_call(q, kd, vd)
                mix = ([oa], flat(ob), p["wo_a"], p["wo_b"])
            else:
                p = odd_weights(e)
                q, k, dd, vt, ot, gt = _odd_in_call(x, mod, gain, p["w"], p["wt"], p["wgt"], p["bg"])
                per_seq_t = lambda a: a.reshape((b, t // CHUNK) + a.shape[2:])
                c0, n0, m0 = ctx_odd[e]
                hf, hb, c1, n1, m1 = _mlstm_call(per_seq(q), per_seq(k), per_seq_t(vt), per_seq_t(ot),
                                                 per_seq_t(gt), c0, n0, m0)
                new_odd.append((c1, n1, m1))
                od = _fourier_call(per_seq(dd), *_fourier_tables(t))
                mix = ([flat(hf), flat(hb)], flat(od), p["wo_a"], p["wo_b"])
            x = _ffn_call(x, mod, gain, ffn_in[l, 1], ffn_out[l, 1], sub=2, mix=mix)
        return x, new_even, new_odd

    n_even = (depth + 1) // 2
    n_odd = depth // 2

    zero_state = (jnp.zeros((batch, N_STREAMS, C_DV, C_DK), F32), jnp.zeros((batch, N_STREAMS, C_DK), F32),
                  jnp.zeros((batch, N_STREAMS, CHUNK), F32))
    y, new_even, new_odd = run_pass(x_prompt.reshape(1, batch * seq, d_model), list(mod_ctx), batch, seq,
                                    None, None, [zero_state] * n_odd, True)
    y_prompt = y.reshape(batch, seq, d_model)
    new_cache_k = jnp.stack([k.reshape(batch, seq, B_KV, HEAD_DIM) for k, _ in new_even], axis=1)
    new_cache_v = jnp.stack([v.reshape(batch, seq, B_KV, HEAD_DIM) for _, v in new_even], axis=1)
    new_state_c = jnp.stack([c1.reshape(batch, N_DIR, C_HEADS, C_DV, C_DK) for c1, _, _ in new_odd], axis=1)
    new_state_n = jnp.stack([n1.reshape(batch, N_DIR, C_HEADS, C_DK) for _, n1, _ in new_odd], axis=1)
    new_state_m = jnp.stack([m1[:, :, 0].reshape(batch, N_DIR, C_HEADS) for _, _, m1 in new_odd], axis=1)

    ctx_even = [(cache_k[:, e], cache_v[:, e]) for e in range(n_even)]
    ctx_odd = []
    for e in range(n_odd):
        m0 = jnp.broadcast_to(state_m[:, e].reshape(dec_batch, N_STREAMS, 1), (dec_batch, N_STREAMS, CHUNK))
        ctx_odd.append((state_C[:, e].reshape(dec_batch, N_STREAMS, C_DV, C_DK),
                        state_n[:, e].reshape(dec_batch, N_STREAMS, C_DK), m0))
    y_sample, _, _ = run_pass(x_sample, list(mod_lat), dec_batch, dec_seq, _rope_tables(dec_seq),
                              ctx_even, ctx_odd, False)

    return (y_prompt, y_sample, new_cache_k, new_cache_v, new_state_c, new_state_n, new_state_m)
```

```python
import functools
import math

import jax
import jax.numpy as jnp
from jax import lax
from jax.experimental import pallas as pl
from jax.experimental.pallas import tpu as pltpu

BF = jnp.bfloat16
F32 = jnp.float32

EPS = 1e-6
ROPE_THETA = 10000.0
GRID_W = 64
CHUNK = 128
N_SUB = 3
A_GROUPS = 4
A_DIM = 128
A_WIDTH = A_GROUPS * A_DIM
B_HEADS = 8
B_KV = 2
HEAD_DIM = 64
B_WIDTH = B_HEADS * HEAD_DIM
C_HEADS = 4
C_DK = 128
C_DV = 128
C_WIDTH = C_HEADS * C_DV
N_DIR = 2
N_STREAMS = N_DIR * C_HEADS
D_GROUPS = 4
D_DIM = 128
D_WIDTH = D_GROUPS * D_DIM
LANES = 128

VMEM_LIMIT = 56 * 1024 * 1024


def _params(*sem):
    return pltpu.CompilerParams(dimension_semantics=sem, vmem_limit_bytes=VMEM_LIMIT)


def _dot(a, b):
    return jnp.dot(a, b, preferred_element_type=F32)


def _dot_nt(a, b):
    return lax.dot_general(a, b, (((1,), (1,)), ((), ())), preferred_element_type=F32)


def _dot_tn(a, b):
    return lax.dot_general(a, b, (((0,), (0,)), ((), ())), preferred_element_type=F32)


def _sigmoid(x):
    return 1.0 / (1.0 + jnp.exp(-x))


def _gelu_tanh(x):
    return 0.5 * x * (1.0 + jnp.tanh(math.sqrt(2.0 / math.pi) * (x + 0.044715 * (x * x * x))))


def _norm_mod(x, gain, shift, scale):
    ms = jnp.mean(x * x, axis=-1, keepdims=True)
    y = x * lax.rsqrt(ms + EPS) * gain
    return y * (1.0 + scale) + shift


def _const_spec(shape):
    zeros = (0,) * len(shape)
    return pl.BlockSpec(shape, lambda *_: zeros, pipeline_mode=pl.Buffered(1))


def _row_tile(t, target):
    tm = min(t, target)
    assert t % tm == 0
    return tm


def _mod_kernel(c_ref, w_ref, b_ref, o_ref):
    c = c_ref[...]
    sc = (c * _sigmoid(c)).astype(BF)
    o_ref[0] = _dot(sc, w_ref[0].astype(BF)) + b_ref[0]


def _mod_call(c_rows, w_mod, b_mod):
    depth, d, n = w_mod.shape
    rows = c_rows.shape[0]
    tn = 1024
    assert n % tn == 0
    return pl.pallas_call(
        _mod_kernel,
        grid=(depth, n // tn),
        in_specs=[
            pl.BlockSpec((rows, d), lambda l, j: (0, 0)),
            pl.BlockSpec((1, d, tn), lambda l, j: (l, 0, j)),
            pl.BlockSpec((1, 1, tn), lambda l, j: (l, 0, j)),
        ],
        out_specs=pl.BlockSpec((1, rows, tn), lambda l, j: (l, 0, j)),
        out_shape=jax.ShapeDtypeStruct((depth, rows, n), F32),
        compiler_params=_params("parallel", "parallel"),
        name="mod",
    )(c_rows, w_mod, b_mod.reshape(depth, 1, n))


FFN_CHUNK = 256


def _ffn_kernel(*refs, n_mix_a, sub):
    if n_mix_a:
        x_ref = refs[0]
        a_refs = refs[1:1 + n_mix_a]
        b_ref, mod_ref, gain_ref, wa_ref, wb_ref, win_ref, wout_ref, o_ref, act_ref = refs[1 + n_mix_a:]
    else:
        x_ref, mod_ref, gain_ref, win_ref, wout_ref, o_ref, act_ref = refs
    x = x_ref[0]
    if n_mix_a:
        if n_mix_a == 1:
            a = a_refs[0][0]
        else:
            a = (a_refs[0][0].astype(F32) + a_refs[1][0].astype(F32)).astype(BF)
        mix = _dot(a, wa_ref[...]) + _dot(b_ref[0], wb_ref[...])
        x = x + mod_ref[0, 5:6, :] * mix
    h = _norm_mod(x, gain_ref[sub:sub + 1, :], mod_ref[0, 3 * sub:3 * sub + 1, :],
                  mod_ref[0, 3 * sub + 1:3 * sub + 2, :]).astype(BF)
    d_ff = act_ref.shape[1]
    for c in range(d_ff // FFN_CHUNK):
        lo = c * FFN_CHUNK
        g = _dot(h, win_ref[:, lo:lo + FFN_CHUNK])
        u = _dot(h, win_ref[:, d_ff + lo:d_ff + lo + FFN_CHUNK])
        act_ref[:, lo:lo + FFN_CHUNK] = (g * _sigmoid(g) * u).astype(BF)
    y = _dot(act_ref[...], wout_ref[...])
    o_ref[0] = x + (0.5 * mod_ref[0, 3 * sub + 2:3 * sub + 3, :]) * y


def _ffn_call(x, mod, gain, w_in, w_out, layer, slot, sub, mix=None):
    bm, t, d = x.shape
    d_ff = w_out.shape[2]
    weight = lambda w: pl.BlockSpec((None, None) + w.shape[2:], lambda b, i: (layer, slot, 0, 0),
                                    pipeline_mode=pl.Buffered(1))
    tm = _row_tile(t, 512)
    row = lambda w: pl.BlockSpec((1, tm, w), lambda b, i: (b, i, 0))
    in_specs = [row(d)]
    args = [x]
    n_mix_a = 0
    if mix is not None:
        a_list, b_arr, w_a, w_b = mix
        n_mix_a = len(a_list)
        for a in a_list:
            in_specs.append(row(a.shape[-1]))
            args.append(a)
        in_specs.append(row(b_arr.shape[-1]))
        args.append(b_arr)
    in_specs += [pl.BlockSpec((1, 3 * N_SUB, d), lambda b, i: (b, 0, 0)), _const_spec(gain.shape)]
    args += [mod, gain]
    if mix is not None:
        in_specs += [_const_spec(w_a.shape), _const_spec(w_b.shape)]
        args += [w_a, w_b]
    in_specs += [weight(w_in), weight(w_out)]
    args += [w_in, w_out]
    return pl.pallas_call(
        functools.partial(_ffn_kernel, n_mix_a=n_mix_a, sub=sub),
        grid=(bm, t // tm),
        in_specs=in_specs,
        out_specs=row(d),
        out_shape=jax.ShapeDtypeStruct((bm, t, d), F32),
        scratch_shapes=[pltpu.VMEM((tm, d_ff), BF)],
        compiler_params=_params("parallel", "parallel"),
        name="ffn_mix" if mix is not None else "ffn",
    )(*args)


def _rope(x, cos, sin_signed):
    w = x.shape[1]
    n = w // LANES
    cos = jnp.concatenate([cos] * n, axis=1)
    sin_signed = jnp.concatenate([sin_signed] * n, axis=1)
    lane = lax.broadcasted_iota(jnp.int32, (1, w), 1)
    first_half = (lane & (HEAD_DIM - 1)) < (HEAD_DIM // 2)
    fwd = pltpu.roll(x, w - HEAD_DIM // 2, axis=1)
    bwd = pltpu.roll(x, HEAD_DIM // 2, axis=1)
    return x * cos + jnp.where(first_half, fwd, bwd) * sin_signed


def _even_in_kernel(*refs, has_rope, is_ctx):
    it = iter(refs)
    x_ref, mod_ref, gain_ref, w_ref, sgu_ref, ws_ref, bs_ref, qg_ref, kg_ref, bd_ref = (next(it) for _ in range(10))
    if has_rope:
        cos_ref, sin_ref = next(it), next(it)
    oa_ref, q_ref, kd_ref, vd_ref = (next(it) for _ in range(4))
    if is_ctx:
        k32_ref, v32_ref = next(it), next(it)

    tm = x_ref.shape[1]
    nck = tm // CHUNK
    x = x_ref[0]
    h = _norm_mod(x, gain_ref[1:2, :], mod_ref[0, 3:4, :], mod_ref[0, 4:5, :]).astype(BF)

    au = _gelu_tanh(_dot(h, w_ref[:, 0:A_WIDTH]))
    av = _gelu_tanh(_dot(h, w_ref[:, A_WIDTH:2 * A_WIDTH]))
    for g in range(A_GROUPS):
        cols = slice(g * A_DIM, (g + 1) * A_DIM)
        blk = av[:, cols]
        ms = jnp.mean(blk * blk, axis=-1, keepdims=True)
        vg = (blk * lax.rsqrt(ms + EPS) * sgu_ref[:, cols]).astype(BF)
        rhs = jnp.concatenate([vg[c * CHUNK:(c + 1) * CHUNK, :] for c in range(nck)], axis=1)
        s = _dot(ws_ref[g], rhs)
        for c in range(nck):
            rows = slice(c * CHUNK, (c + 1) * CHUNK)
            gate = s[:, c * A_DIM:(c + 1) * A_DIM] + bs_ref[g]
            oa_ref[0, rows, cols] = (au[rows, cols] * gate).astype(BF)

    lane = lax.broadcasted_iota(jnp.int32, (1, LANES), 1)
    lo_half = lane < HEAD_DIM
    if has_rope:
        cos = cos_ref[...]
        sin = sin_ref[...]

    off = 2 * A_WIDTH
    q = _dot(h, w_ref[:, off:off + B_WIDTH])
    ms = _dot((q * q).astype(BF), bd_ref[...])
    q = q * lax.rsqrt(ms + EPS) * qg_ref[...]
    if has_rope:
        q = _rope(q, cos, sin)
    q_ref[0] = (q * (HEAD_DIM ** -0.5)).astype(BF)

    off += B_WIDTH
    kvw = 2 * B_KV * HEAD_DIM
    k = _dot(h, w_ref[:, off:off + kvw])
    ms = _dot((k * k).astype(BF), bd_ref[0:kvw, 0:kvw])
    k = k * lax.rsqrt(ms + EPS) * kg_ref[...]
    if is_ctx:
        k32_ref[0] = jnp.where(lo_half, k[:, 0:LANES], k[:, LANES:2 * LANES])
    if has_rope:
        k = _rope(k, cos, sin)
    kd_ref[0] = k.astype(BF)
    off += kvw
    v = _dot(h, w_ref[:, off:off + kvw])
    if is_ctx:
        v32_ref[0] = jnp.where(lo_half, v[:, 0:LANES], v[:, LANES:2 * LANES])
    vd_ref[0] = v.astype(BF)


def _even_in_call(x, mod, gain, w, sgu, ws, bs, qg, kg, bd, rope, is_ctx):
    bm, t, d = x.shape
    tm = _row_tile(t, 512)
    kvw = 2 * B_KV * HEAD_DIM
    row = lambda w_: pl.BlockSpec((1, tm, w_), lambda b, i: (b, i, 0))
    in_specs = [row(d), pl.BlockSpec((1, 3 * N_SUB, d), lambda b, i: (b, 0, 0))]
    in_specs += [_const_spec(a.shape) for a in (gain, w, sgu, ws, bs, qg, kg, bd)]
    args = [x, mod, gain, w, sgu, ws, bs, qg, kg, bd]
    if rope is not None:
        in_specs += [pl.BlockSpec((tm, LANES), lambda b, i: (i, 0))] * 2
        args += list(rope)
    out_specs = [row(A_WIDTH), row(B_WIDTH), row(kvw), row(kvw)]
    out_shape = [jax.ShapeDtypeStruct((bm, t, A_WIDTH), BF), jax.ShapeDtypeStruct((bm, t, B_WIDTH), BF),
                 jax.ShapeDtypeStruct((bm, t, kvw), BF), jax.ShapeDtypeStruct((bm, t, kvw), BF)]
    if is_ctx:
        out_specs += [row(B_KV * HEAD_DIM)] * 2
        out_shape += [jax.ShapeDtypeStruct((bm, t, B_KV * HEAD_DIM), F32)] * 2
    return pl.pallas_call(
        functools.partial(_even_in_kernel, has_rope=rope is not None, is_ctx=is_ctx),
        grid=(bm, t // tm),
        in_specs=in_specs,
        out_specs=out_specs,
        out_shape=out_shape,
        compiler_params=_params("parallel", "parallel"),
        name="even_in",
    )(*args)


ATTN_ROWS = 128


def _attn_kernel(*refs, n_kv):
    q_ref = refs[0]
    k_refs = refs[1:1 + n_kv]
    v_refs = refs[1 + n_kv:1 + 2 * n_kv]
    o_ref = refs[1 + 2 * n_kv]
    tq = q_ref.shape[1]
    lane = lax.broadcasted_iota(jnp.int32, (1, LANES), 1)
    lo_half = lane < HEAD_DIM
    heads_per_kv = B_HEADS // B_KV
    units = [(r, h) for r in range(tq // ATTN_ROWS) for h in range(B_HEADS)]

    def kv_cols(h):
        kv = h // heads_per_kv
        return slice(kv * LANES, (kv + 1) * LANES)

    def scores(unit):
        r, h = unit
        p = h // 2
        qp = q_ref[0, r * ATTN_ROWS:(r + 1) * ATTN_ROWS, p * LANES:(p + 1) * LANES]
        keep = lo_half if h % 2 == 0 else jnp.logical_not(lo_half)
        qm = jnp.where(keep, qp, jnp.zeros_like(qp))
        return [_dot_nt(qm, k_ref[0, :, kv_cols(h)]) for k_ref in k_refs]

    def weights(ss):
        m = functools.reduce(jnp.maximum, [jnp.max(s, axis=-1, keepdims=True) for s in ss])
        es = [jnp.exp(s - m) for s in ss]
        return [e.astype(BF) for e in es], sum(jnp.sum(e, axis=-1, keepdims=True) for e in es)

    def attend(es, l, unit):
        return sum(_dot(e, v_ref[0, :, kv_cols(unit[1])]) for e, v_ref in zip(es, v_refs)) / l

    outs = {}
    s_next = scores(units[0])
    pending = None
    for i, unit in enumerate(units):
        s = s_next
        if i + 1 < len(units):
            s_next = scores(units[i + 1])
        e, l = weights(s)
        if pending is not None:
            outs[pending[2]] = attend(*pending)
        pending = (e, l, unit)
    outs[pending[2]] = attend(*pending)
    for r in range(tq // ATTN_ROWS):
        for p in range(B_WIDTH // LANES):
            o_ref[0, r * ATTN_ROWS:(r + 1) * ATTN_ROWS, p * LANES:(p + 1) * LANES] = jnp.where(
                lo_half, outs[(r, 2 * p)], outs[(r, 2 * p + 1)]).astype(BF)


def _attn_call(q, k_blocks, v_blocks):
    b, t, w = q.shape
    tq = _row_tile(t, 512)
    whole = lambda a: pl.BlockSpec((1,) + a.shape[1:], lambda i, j: (i, 0, 0))
    return pl.pallas_call(
        functools.partial(_attn_kernel, n_kv=len(k_blocks)),
        grid=(b, t // tq),
        in_specs=[pl.BlockSpec((1, tq, w), lambda i, j: (i, j, 0))]
        + [whole(a) for a in k_blocks] + [whole(a) for a in v_blocks],
        out_specs=pl.BlockSpec((1, tq, w), lambda i, j: (i, j, 0)),
        out_shape=jax.ShapeDtypeStruct((b, t, w), BF),
        compiler_params=_params("parallel", "parallel"),
        name="attn",
    )(q, *k_blocks, *v_blocks)


N_GATE_ROWS = 2 * N_STREAMS


def _log_sigmoid(x):
    return jnp.minimum(x, 0.0) - jnp.log(1.0 + jnp.exp(-jnp.abs(x)))


def _odd_in_kernel(x_ref, mod_ref, gain_ref, w_ref, wt_ref, wgt_ref, bg_ref,
                   q_ref, k_ref, d_ref, vt_ref, ot_ref, gt_ref):
    tm = x_ref.shape[1]
    x = x_ref[0]
    h = _norm_mod(x, gain_ref[1:2, :], mod_ref[0, 3:4, :], mod_ref[0, 4:5, :]).astype(BF)
    w = C_WIDTH
    q_ref[0] = _dot(h, w_ref[:, 0:w]).astype(BF)
    k_ref[0] = (_dot(h, w_ref[:, w:2 * w]) * (C_DK ** -0.5)).astype(BF)
    d_ref[0] = _dot(h, w_ref[:, 2 * w:2 * w + D_WIDTH]).astype(BF)
    vt = _dot_nt(wt_ref[0:w, :], h)
    ot = _sigmoid(_dot_nt(wt_ref[w:2 * w, :], h))
    gt = _dot_nt(wgt_ref[...], h)
    row = lax.broadcasted_iota(jnp.int32, (N_GATE_ROWS, CHUNK), 0)
    for c in range(tm // CHUNK):
        lanes = slice(c * CHUNK, (c + 1) * CHUNK)
        vt_ref[0, c] = vt[:, lanes].astype(BF)
        ot_ref[0, c] = ot[:, lanes].astype(BF)
        g = gt[:, lanes] + bg_ref[...]
        gt_ref[0, c] = jnp.where(row >= N_STREAMS, _log_sigmoid(g), g)


def _odd_in_call(x, mod, gain, w, wt, wgt, bg):
    bm, t, d = x.shape
    tm = _row_tile(t, 512)
    nck = tm // CHUNK
    row = lambda w_: pl.BlockSpec((1, tm, w_), lambda b, i: (b, i, 0))
    chunked = lambda r: pl.BlockSpec((1, nck, r, CHUNK), lambda b, i: (b, i, 0, 0))
    return pl.pallas_call(
        _odd_in_kernel,
        grid=(bm, t // tm),
        in_specs=[row(d), pl.BlockSpec((1, 3 * N_SUB, d), lambda b, i: (b, 0, 0))]
        + [_const_spec(a.shape) for a in (gain, w, wt, wgt, bg)],
        out_specs=[row(C_WIDTH), row(C_WIDTH), row(D_WIDTH), chunked(C_WIDTH), chunked(C_WIDTH),
                   chunked(N_GATE_ROWS)],
        out_shape=[jax.ShapeDtypeStruct((bm, t, C_WIDTH), BF), jax.ShapeDtypeStruct((bm, t, C_WIDTH), BF),
                   jax.ShapeDtypeStruct((bm, t, D_WIDTH), BF),
                   jax.ShapeDtypeStruct((bm, t // CHUNK, C_WIDTH, CHUNK), BF),
                   jax.ShapeDtypeStruct((bm, t // CHUNK, C_WIDTH, CHUNK), BF),
                   jax.ShapeDtypeStruct((bm, t // CHUNK, N_GATE_ROWS, CHUNK), F32)],
        compiler_params=_params("parallel", "parallel"),
        name="odd_in",
    )(x, mod, gain, w, wt, wgt, bg)


N_REP = 8
CN_ROWS = C_DV + N_REP


def _split3(x):
    hi = x.astype(BF)
    r1 = x - hi.astype(F32)
    mid = r1.astype(BF)
    lo = (r1 - mid.astype(F32)).astype(BF)
    return hi, mid, lo


def _mlstm_kernel(qf_ref, kf_ref, vtf_ref, otf_ref, gtf_ref, qb_ref, kb_ref, vtb_ref, otb_ref, gtb_ref,
                  c0_ref, n0_ref, m0_ref,
                  hf_ref, hb_ref, c1_ref, n1_ref, m1_ref,
                  cn_s, m_s):
    i = pl.program_id(1)
    nck = qf_ref.shape[1] // CHUNK

    @pl.when(i == 0)
    def _():
        for sl in range(N_STREAMS):
            cn_s[sl, 0:C_DV, :] = c0_ref[0, sl]
            cn_s[sl, C_DV:CN_ROWS, :] = jnp.broadcast_to(n0_ref[0, sl:sl + 1, :], (N_REP, C_DK))
        m_s[...] = m0_ref[0]

    r_i = lax.broadcasted_iota(jnp.int32, (CHUNK, CHUNK), 0)
    c_i = lax.broadcasted_iota(jnp.int32, (CHUNK, CHUNK), 1)
    upper = c_i >= r_i
    lower = c_i <= r_i
    tri_up = jnp.where(upper, 1.0, 0.0).astype(BF)
    tri_lo = jnp.where(lower, 1.0, 0.0).astype(BF)
    eye = jnp.where(c_i == r_i, 1.0, 0.0).astype(BF)
    ones = jnp.ones((CHUNK, CHUNK), BF)
    n_rows = nck * N_STREAMS
    assert n_rows <= CHUNK
    is_fwd = (lax.broadcasted_iota(jnp.int32, (n_rows, CHUNK), 0) & (N_STREAMS - 1)) < C_HEADS
    is_fwd8 = is_fwd[0:N_STREAMS]
    lane = lax.broadcasted_iota(jnp.int32, (n_rows, CHUNK), 1)

    g_i, g_f = [], []
    for c in range(nck):
        gf = gtf_ref[0, c]
        gb = gtb_ref[0, nck - 1 - c]
        g_i.append(jnp.where(is_fwd8, gf[0:N_STREAMS], gb[0:N_STREAMS]))
        g_f.append(jnp.where(is_fwd8, gf[N_STREAMS:N_GATE_ROWS], gb[N_STREAMS:N_GATE_ROWS]))
    g_i = jnp.concatenate(g_i, axis=0)
    g_f = jnp.concatenate(g_f, axis=0)
    parts = _split3(g_f)
    cum = jnp.where(is_fwd, sum(_dot(p, tri_up) for p in parts),
                    sum(_dot(p, tri_lo) for p in parts))
    total = sum(_dot(p, ones) for p in parts)
    r = g_i - cum
    run = r
    sh = 1
    while sh < CHUNK:
        prev = jnp.where(lane >= sh, pltpu.roll(run, sh, axis=1), -jnp.inf)
        nxt = jnp.where(lane < CHUNK - sh, pltpu.roll(run, CHUNK - sh, axis=1), -jnp.inf)
        run = jnp.maximum(run, jnp.where(is_fwd, prev, nxt))
        sh *= 2
    log_s = total - cum + g_i
    ls_max = jnp.max(log_s, axis=-1, keepdims=True)
    m = m_s[...]
    m_old, m_new = [], []
    for c in range(nck):
        rows = slice(c * N_STREAMS, (c + 1) * N_STREAMS)
        m_old.append(m)
        m = jnp.maximum(total[rows] + m, ls_max[rows])
        m_new.append(m)
    m_s[...] = m
    m_old = jnp.concatenate(m_old, axis=0)
    m_new = jnp.concatenate(m_new, axis=0)
    log_inter = cum + m_old
    m_t = jnp.maximum(log_inter, cum + run)
    bm = cum - m_t
    w_prev = jnp.exp(log_inter - m_t)
    e_inv = jnp.exp(-m_t)
    w_src = jnp.exp(log_s - m_new)
    w_keep = jnp.exp(total + m_old - m_new)
    r_cols = jnp.concatenate([r, jnp.zeros((CHUNK - n_rows, CHUNK), F32)], axis=0).T

    row = lambda a, j: a[j:j + 1, :]
    units = []
    for c in range(nck):
        for sl in range(N_STREAMS):
            fwd = sl < C_HEADS
            q_ref, k_ref, vt_ref, ot_ref, h_ref = ((qf_ref, kf_ref, vtf_ref, otf_ref, hf_ref) if fwd
                                                   else (qb_ref, kb_ref, vtb_ref, otb_ref, hb_ref))
            ck = c if fwd else nck - 1 - c
            rows = slice(ck * CHUNK, (ck + 1) * CHUNK)
            cols = slice((sl % C_HEADS) * C_DK, (sl % C_HEADS + 1) * C_DK)
            units.append((c * N_STREAMS + sl, sl, q_ref[0, rows, cols], k_ref[0, rows, cols],
                          vt_ref[0, ck, cols, :], ot_ref[0, ck, cols, :], h_ref, rows, cols))
    s_t = [_dot_nt(k, q) for _, _, q, k, _, _, _, _, _ in units]
    upd = []
    for j, _, _, k, vt, _, _, _, _ in units:
        ws = row(w_src, j)
        lhs = jnp.concatenate([(vt.astype(F32) * ws).astype(BF),
                               jnp.broadcast_to(ws, (N_REP, CHUNK)).astype(BF)], axis=0)
        upd.append(_dot(lhs, k))
    a_t = []
    for (j, sl, *_), s in zip(units, s_t):
        mask = upper if sl < C_HEADS else lower
        a_t.append(jnp.exp(jnp.where(mask, r_cols[:, j:j + 1] + row(bm, j), -jnp.inf)) * s)
    num = [_dot(u[4], a.astype(BF)) for u, a in zip(units, a_t)]
    cn = [cn_s[sl] for sl in range(N_STREAMS)]
    step_units = lambda c: range(c * N_STREAMS, (c + 1) * N_STREAMS)

    def read_state(c):
        out = []
        for idx in step_units(c):
            j, sl, q = units[idx][0:3]
            out.append(_dot_nt(cn[sl].astype(BF), q))
            cn[sl] = row(w_keep, j) * cn[sl] + upd[idx]
        return out

    inter = read_state(0)
    for c in range(nck):
        h_t = []
        for idx, it in zip(step_units(c), inter):
            j, ot = units[idx][0], units[idx][5]
            wp = row(w_prev, j)
            den = jnp.sum(a_t[idx], axis=0, keepdims=True) + wp * it[C_DV:C_DV + 1]
            denom = jnp.maximum(jnp.abs(den), row(e_inv, j))
            h_t.append(((num[idx] + wp * it[0:C_DV]) * (1.0 / denom) * ot.astype(F32)).astype(BF))
        if c + 1 < nck:
            inter = read_state(c + 1)
        h_rows = [_dot_nt(eye, h).astype(BF) for h in h_t]
        for idx, h in zip(step_units(c), h_rows):
            h_ref, rows, cols = units[idx][6:9]
            h_ref[0, rows, cols] = h
    for sl in range(N_STREAMS):
        cn_s[sl] = cn[sl]

    @pl.when(i == pl.num_programs(1) - 1)
    def _():
        for sl in range(N_STREAMS):
            c1_ref[0, sl] = cn_s[sl, 0:C_DV, :]
            n1_ref[0, sl:sl + 1, :] = cn_s[sl, C_DV:C_DV + 1, :]
        m1_ref[0] = m_s[...]


def _mlstm_call(q, k, vt, ot, gt, c0, n0, m0):
    b, t, w = q.shape
    tb = _row_tile(t, 512)
    nt = t // tb
    nck = tb // CHUNK
    fwd = pl.BlockSpec((1, tb, w), lambda i, j: (i, j, 0))
    bwd = pl.BlockSpec((1, tb, w), lambda i, j: (i, nt - 1 - j, 0))
    fwd_t = lambda r: pl.BlockSpec((1, nck, r, CHUNK), lambda i, j: (i, j, 0, 0))
    bwd_t = lambda r: pl.BlockSpec((1, nck, r, CHUNK), lambda i, j: (i, nt - 1 - j, 0, 0))
    st_c = pl.BlockSpec((1, N_STREAMS, C_DV, C_DK), lambda i, j: (i, 0, 0, 0))
    st_n = pl.BlockSpec((1, N_STREAMS, C_DK), lambda i, j: (i, 0, 0))
    st_m = pl.BlockSpec((1, N_STREAMS, CHUNK), lambda i, j: (i, 0, 0))
    return pl.pallas_call(
        _mlstm_kernel,
        grid=(b, nt),
        in_specs=[fwd, fwd, fwd_t(w), fwd_t(w), fwd_t(N_GATE_ROWS), bwd, bwd, bwd_t(w), bwd_t(w),
                  bwd_t(N_GATE_ROWS), st_c, st_n, st_m],
        out_specs=[fwd, bwd, st_c, st_n, st_m],
        out_shape=[jax.ShapeDtypeStruct((b, t, w), BF), jax.ShapeDtypeStruct((b, t, w), BF),
                   jax.ShapeDtypeStruct(c0.shape, F32), jax.ShapeDtypeStruct(n0.shape, F32),
                   jax.ShapeDtypeStruct(m0.shape, F32)],
        scratch_shapes=[pltpu.VMEM((N_STREAMS, CN_ROWS, C_DK), F32), pltpu.VMEM((N_STREAMS, CHUNK), F32)],
        compiler_params=_params("parallel", "arbitrary"),
        name="mlstm",
    )(q, k, vt, ot, gt, q, k, vt, ot, gt, c0, n0, m0)


def _fourier_kernel(d_ref, dft_t_ref, dft_c_ref, o_ref, y_s, *, scale):
    t = d_ref.shape[1]

    @pl.when(pl.program_id(1) == 0)
    def _():
        step = min(t, 512)
        for r in range(t // step):
            rows = slice(r * step, (r + 1) * step)
            for g in range(D_GROUPS):
                cols = slice(g * D_DIM, (g + 1) * D_DIM)
                y = _dot(d_ref[0, rows, cols], dft_c_ref[...])
                y_s[rows, cols] = y[:, 0:D_DIM].astype(BF)
                y_s[t + r * step:t + (r + 1) * step, cols] = y[:, D_DIM:2 * D_DIM].astype(BF)

    o_ref[0] = (_dot(dft_t_ref[...], y_s[...]) * scale).astype(BF)


def _fourier_call(d, dft_t, dft_c):
    b, t, w = d.shape
    tr = _row_tile(t, 512)
    return pl.pallas_call(
        functools.partial(_fourier_kernel, scale=float((t * D_DIM) ** -0.5)),
        grid=(b, t // tr),
        in_specs=[
            pl.BlockSpec((1, t, w), lambda i, j: (i, 0, 0)),
            pl.BlockSpec((tr, 2 * t), lambda i, j: (j, 0)),
            _const_spec(dft_c.shape),
        ],
        out_specs=pl.BlockSpec((1, tr, w), lambda i, j: (i, j, 0)),
        out_shape=jax.ShapeDtypeStruct((b, t, w), BF),
        scratch_shapes=[pltpu.VMEM((2 * t, w), BF)],
        compiler_params=_params("parallel", "arbitrary"),
        name="fourier",
    )(d, dft_t, dft_c)


DFT_LANES = 4096
DFT_RESIDUES = 8


def _dft_rows_kernel(x_ref, f_ref, zr_ref, zi_ref):
    n1 = x_ref.shape[1]
    z = _dot(f_ref[...], x_ref[0])
    zr_ref[0] = z[0:n1].astype(BF)
    zi_ref[0] = z[n1:2 * n1].astype(BF)


def _dft_rows_call(xv, f1):
    b, n1, n = xv.shape
    tc = min(n, DFT_LANES)
    spec = pl.BlockSpec((1, n1, tc), lambda i, j: (i, 0, j))
    return pl.pallas_call(
        _dft_rows_kernel,
        grid=(b, n // tc),
        in_specs=[spec, _const_spec(f1.shape)],
        out_specs=[spec, spec],
        out_shape=[jax.ShapeDtypeStruct(xv.shape, BF)] * 2,
        compiler_params=_params("parallel", "parallel"),
        name="dft_rows",
    )(xv, f1)


def _dft_cols_kernel(zr_ref, zi_ref, g_ref, c_ref, o_ref, *, scale):
    n2 = GRID_W
    nb = zr_ref.shape[1] // n2
    w = zr_ref.shape[2]
    xr, xi = [], []
    for r in range(nb):
        rows = slice(r * n2, (r + 1) * n2)
        z = jnp.concatenate([zr_ref[0, rows, :], zi_ref[0, rows, :]], axis=0)
        x = _dot(g_ref[r], z)
        xr.append(x[0:n2])
        xi.append(x[n2:2 * n2])
    xr = jnp.concatenate(xr, axis=0).astype(BF)
    xi = jnp.concatenate(xi, axis=0).astype(BF)
    for g in range(D_GROUPS):
        cols = slice(g * D_DIM, (g + 1) * D_DIM)
        y = _dot(jnp.concatenate([xr[:, cols], xi[:, cols]], axis=1), c_ref[...]) * scale
        for r in range(nb):
            o_ref[0, :, r * w + g * D_DIM:r * w + (g + 1) * D_DIM] = y[r * n2:(r + 1) * n2].astype(BF)


def _dft_cols_call(zr, zi, gmat, cmat, scale):
    b, t, w = zr.shape
    n2 = GRID_W
    n1 = t // n2
    nb = DFT_RESIDUES
    zspec = pl.BlockSpec((1, nb * n2, w), lambda i, j: (i, j, 0))
    return pl.pallas_call(
        functools.partial(_dft_cols_kernel, scale=scale),
        grid=(b, n1 // nb),
        in_specs=[zspec, zspec, pl.BlockSpec((nb, 2 * n2, 2 * n2), lambda i, j: (j, 0, 0)),
                  _const_spec(cmat.shape)],
        out_specs=pl.BlockSpec((1, n2, nb * w), lambda i, j: (i, 0, j)),
        out_shape=jax.ShapeDtypeStruct((b, n2, n1 * w), BF),
        compiler_params=_params("parallel", "parallel"),
        name="dft_cols",
    )(zr, zi, gmat, cmat)


def _cos_sin(num, den):
    ang = (num % den).astype(F32) * (2.0 * math.pi / den)
    return jnp.cos(ang), jnp.sin(ang)


def _dft_cos_sin(n):
    k = jnp.arange(n, dtype=jnp.int32)
    return _cos_sin(k[:, None] * k[None, :], n)


DENSE_DFT_MAX = 1024


def _fourier(d):
    b, t, w = d.shape
    cc, sc = _dft_cos_sin(D_DIM)
    scale = float((t * D_DIM) ** -0.5)
    if t <= DENSE_DFT_MAX:
        ct, st = _dft_cos_sin(t)
        return _fourier_call(d, jnp.concatenate([ct, -st], axis=1).astype(BF),
                             jnp.concatenate([cc, sc], axis=1).astype(BF))
    n2 = GRID_W
    n1 = t // n2
    assert n1 * n2 == t and n1 % DFT_RESIDUES == 0
    a1 = jnp.arange(n1, dtype=jnp.int32)
    a2 = jnp.arange(n2, dtype=jnp.int32)
    c1, s1 = _cos_sin(a1[:, None] * a1[None, :], n1)
    f1 = jnp.concatenate([c1, -s1], axis=0).astype(BF)
    tp = a1[:, None, None] + n1 * a2[None, :, None]
    gc, gs = _cos_sin(tp * a2[None, None, :], t)
    gmat = jnp.concatenate([jnp.concatenate([gc, gs], axis=2),
                            jnp.concatenate([-gs, gc], axis=2)], axis=1).astype(BF)
    cmat = jnp.concatenate([cc, sc], axis=0).astype(BF)
    zr, zi = _dft_rows_call(d.reshape(b, n1, n2 * w), f1)
    out = _dft_cols_call(zr.reshape(b, t, w), zi.reshape(b, t, w), gmat, cmat, scale)
    return out.reshape(b, t, w)


def _rope_tables(t):
    rows = t // GRID_W
    r = jnp.repeat(jnp.arange(rows), GRID_W).astype(F32)
    cidx = jnp.tile(jnp.arange(GRID_W), rows).astype(F32)
    n_freq = HEAD_DIM // 4
    inv = ROPE_THETA ** (-jnp.arange(n_freq, dtype=F32) / n_freq)
    ang = jnp.concatenate([r[:, None] * inv, cidx[:, None] * inv], axis=-1)
    ang = jnp.concatenate([ang] * (2 * LANES // HEAD_DIM), axis=-1)
    lane = jnp.arange(LANES)
    sign = jnp.where((lane % HEAD_DIM) < HEAD_DIM // 2, -1.0, 1.0).astype(F32)
    return jnp.cos(ang), jnp.sin(ang) * sign


def _dup_heads(a):
    parts = []
    for h in range(B_KV):
        blk = a[..., h * HEAD_DIM:(h + 1) * HEAD_DIM]
        parts += [blk, blk]
    return jnp.concatenate(parts, axis=-1)


def kernel(x_prompt, x_sample, c, cache_k, cache_v, state_C, state_n, state_m, c_ctx, norm_gain, w_mod, b_mod,
           ffn_w_in, ffn_w_out, w_in_even, w_out_even, spatial_w, spatial_b, sgu_gain, q_gain, k_gain,
           w_in_odd, b_gate_odd, w_out_odd):
    depth = norm_gain.shape[0]
    batch, seq, d_model = x_prompt.shape
    dec_batch, dec_seq, _ = x_sample.shape

    rows = 1 + dec_batch
    rows_pad = -(-rows // 8) * 8
    c_rows = jnp.concatenate([c_ctx[None, :], c, jnp.zeros((rows_pad - rows, d_model), F32)], axis=0)
    mod_all = _mod_call(c_rows, w_mod, b_mod)
    mod_ctx = mod_all[:, 0:1].reshape(depth, 1, 3 * N_SUB, d_model)
    mod_lat = mod_all[:, 1:rows].reshape(depth, dec_batch, 3 * N_SUB, d_model)

    ffn_in = ffn_w_in.astype(BF)
    ffn_out = ffn_w_out.astype(BF)

    kv = B_KV * HEAD_DIM
    lane = jnp.arange(LANES)
    bd = ((jnp.arange(B_WIDTH)[:, None] // HEAD_DIM) == (jnp.arange(B_WIDTH)[None, :] // HEAD_DIM))
    bd = jnp.where(bd, 1.0 / HEAD_DIM, 0.0).astype(BF)

    def even_weights(e):
        w = w_in_even[e]
        o = 2 * A_WIDTH + B_WIDTH
        w_cat = jnp.concatenate([w[:, :o], _dup_heads(w[:, o:o + kv]), _dup_heads(w[:, o + kv:o + 2 * kv])], axis=1)
        bs = jnp.broadcast_to(spatial_b[e][:, :, None], (A_GROUPS, CHUNK, A_DIM)).astype(F32)
        return dict(
            w=w_cat.astype(BF), sgu=sgu_gain[e].reshape(1, A_WIDTH), ws=spatial_w[e].astype(BF), bs=bs,
            qg=jnp.tile(q_gain[e], B_HEADS)[None, :], kg=jnp.tile(k_gain[e], 2 * B_KV)[None, :],
            wo_a=w_out_even[e][:A_WIDTH].astype(BF), wo_b=w_out_even[e][A_WIDTH:].astype(BF))

    def odd_weights(e):
        w = w_in_odd[e]
        o = 2 * C_HEADS * C_DK + 2 * C_WIDTH
        n_g = 2 * N_STREAMS
        qk = 2 * C_HEADS * C_DK
        w_rows = jnp.concatenate([w[:, :qk], w[:, o + n_g:]], axis=1)
        w_t = w[:, qk:o].T
        perm = jnp.array([d * 2 * C_HEADS + g * C_HEADS + h
                          for g in range(2) for d in range(N_DIR) for h in range(C_HEADS)])
        wg_t = w[:, o:o + n_g][:, perm].T
        bg = jnp.broadcast_to(b_gate_odd[e].reshape(-1)[perm][:, None], (n_g, CHUNK)).astype(F32)
        return dict(w=w_rows.astype(BF), wt=w_t.astype(BF), wgt=wg_t.astype(BF), bg=bg,
                    wo_a=w_out_odd[e][:C_WIDTH].astype(BF), wo_b=w_out_odd[e][C_WIDTH:].astype(BF))

    def run_pass(x, mods, b, t, rope, ctx_even, ctx_odd, is_ctx):
        bm = x.shape[0]
        per_seq = lambda a: a.reshape(b, t, a.shape[-1])
        flat = lambda a: a.reshape(bm, -1, a.shape[-1])
        new_even, new_odd = [], []
        for l in range(depth):
            e = l // 2
            mod, gain = mods[l], norm_gain[l]
            x = _ffn_call(x, mod, gain, ffn_in, ffn_out, l, 0, sub=0)
            if l % 2 == 0:
                p = even_weights(e)
                outs = _even_in_call(x, mod, gain, p["w"], p["sgu"], p["ws"], p["bs"], p["qg"], p["kg"], bd,
                                     rope, is_ctx)
                oa, q, kd, vd = outs[:4]
                k_blocks, v_blocks = [per_seq(kd)], [per_seq(vd)]
                if is_ctx:
                    new_even.append((outs[4], outs[5]))
                else:
                    ck, cv = ctx_even[e]
                    k_blocks.insert(0, _dup_heads(ck.reshape(b, -1, kv)).astype(BF))
                    v_blocks.insert(0, _dup_heads(cv.reshape(b, -1, kv)).astype(BF))
                ob = _attn_call(per_seq(q), k_blocks, v_blocks)
                mix = ([oa], flat(ob), p["wo_a"], p["wo_b"])
            else:
                p = odd_weights(e)
                q, k, dd, vt, ot, gt = _odd_in_call(x, mod, gain, p["w"], p["wt"], p["wgt"], p["bg"])
                per_seq_t = lambda a: a.reshape((b, t // CHUNK) + a.shape[2:])
                c0, n0, m0 = ctx_odd[e]
                hf, hb, c1, n1, m1 = _mlstm_call(per_seq(q), per_seq(k), per_seq_t(vt), per_seq_t(ot),
                                                 per_seq_t(gt), c0, n0, m0)
                new_odd.append((c1, n1, m1))
                od = _fourier(per_seq(dd))
                mix = ([flat(hf), flat(hb)], flat(od), p["wo_a"], p["wo_b"])
            x = _ffn_call(x, mod, gain, ffn_in, ffn_out, l, 1, sub=2, mix=mix)
        return x, new_even, new_odd

    n_even = (depth + 1) // 2
    n_odd = depth // 2

    zero_state = (jnp.zeros((batch, N_STREAMS, C_DV, C_DK), F32), jnp.zeros((batch, N_STREAMS, C_DK), F32),
                  jnp.zeros((batch, N_STREAMS, CHUNK), F32))
    y, new_even, new_odd = run_pass(x_prompt.reshape(1, batch * seq, d_model), list(mod_ctx), batch, seq,
                                    None, None, [zero_state] * n_odd, True)
    y_prompt = y.reshape(batch, seq, d_model)
    new_cache_k = jnp.stack([k.reshape(batch, seq, B_KV, HEAD_DIM) for k, _ in new_even], axis=1)
    new_cache_v = jnp.stack([v.reshape(batch, seq, B_KV, HEAD_DIM) for _, v in new_even], axis=1)
    new_state_c = jnp.stack([c1.reshape(batch, N_DIR, C_HEADS, C_DV, C_DK) for c1, _, _ in new_odd], axis=1)
    new_state_n = jnp.stack([n1.reshape(batch, N_DIR, C_HEADS, C_DK) for _, n1, _ in new_odd], axis=1)
    new_state_m = jnp.stack([m1[:, :, 0].reshape(batch, N_DIR, C_HEADS) for _, _, m1 in new_odd], axis=1)

    ctx_even = [(cache_k[:, e], cache_v[:, e]) for e in range(n_even)]
    ctx_odd = []
    for e in range(n_odd):
        m0 = jnp.broadcast_to(state_m[:, e].reshape(dec_batch, N_STREAMS, 1), (dec_batch, N_STREAMS, CHUNK))
        ctx_odd.append((state_C[:, e].reshape(dec_batch, N_STREAMS, C_DV, C_DK),
                        state_n[:, e].reshape(dec_batch, N_STREAMS, C_DK), m0))
    y_sample, _, _ = run_pass(x_sample, list(mod_lat), dec_batch, dec_seq, _rope_tables(dec_seq),
                              ctx_even, ctx_odd, False)

    return (y_prompt, y_sample, new_cache_k, new_cache_v, new_state_c, new_state_n, new_state_m)
```

```python
import functools
import math

import jax
import jax.numpy as jnp
from jax import lax
from jax.experimental import pallas as pl
from jax.experimental.pallas import tpu as pltpu

BF = jnp.bfloat16
F32 = jnp.float32

EPS = 1e-6
ROPE_THETA = 10000.0
GRID_W = 64
CHUNK = 128
N_SUB = 3
A_GROUPS = 4
A_DIM = 128
A_WIDTH = A_GROUPS * A_DIM
B_HEADS = 8
B_KV = 2
HEAD_DIM = 64
B_WIDTH = B_HEADS * HEAD_DIM
C_HEADS = 4
C_DK = 128
C_DV = 128
C_WIDTH = C_HEADS * C_DV
N_DIR = 2
N_STREAMS = N_DIR * C_HEADS
D_GROUPS = 4
D_DIM = 128
D_WIDTH = D_GROUPS * D_DIM
LANES = 128

VMEM_LIMIT = 56 * 1024 * 1024


def _params(*sem):
    return pltpu.CompilerParams(dimension_semantics=sem, vmem_limit_bytes=VMEM_LIMIT)


def _dot(a, b):
    return jnp.dot(a, b, preferred_element_type=F32)


def _dot_nt(a, b):
    return lax.dot_general(a, b, (((1,), (1,)), ((), ())), preferred_element_type=F32)


def _dot_tn(a, b):
    return lax.dot_general(a, b, (((0,), (0,)), ((), ())), preferred_element_type=F32)


def _sigmoid(x):
    return 1.0 / (1.0 + jnp.exp(-x))


def _gelu_tanh(x):
    return 0.5 * x * (1.0 + jnp.tanh(math.sqrt(2.0 / math.pi) * (x + 0.044715 * (x * x * x))))


def _norm_mod(x, gain, shift, scale):
    ms = jnp.mean(x * x, axis=-1, keepdims=True)
    y = x * lax.rsqrt(ms + EPS) * gain
    return y * (1.0 + scale) + shift


def _const_spec(shape):
    zeros = (0,) * len(shape)
    return pl.BlockSpec(shape, lambda *_: zeros, pipeline_mode=pl.Buffered(1))


def _row_tile(t, target):
    tm = min(t, target)
    assert t % tm == 0
    return tm


def _mod_kernel(c_ref, w_ref, b_ref, o_ref):
    c = c_ref[...]
    sc = (c * _sigmoid(c)).astype(BF)
    o_ref[0] = _dot(sc, w_ref[0].astype(BF)) + b_ref[0]


def _mod_call(c_rows, w_mod, b_mod):
    depth, d, n = w_mod.shape
    rows = c_rows.shape[0]
    tn = 1024
    assert n % tn == 0
    return pl.pallas_call(
        _mod_kernel,
        grid=(depth, n // tn),
        in_specs=[
            pl.BlockSpec((rows, d), lambda l, j: (0, 0)),
            pl.BlockSpec((1, d, tn), lambda l, j: (l, 0, j)),
            pl.BlockSpec((1, 1, tn), lambda l, j: (l, 0, j)),
        ],
        out_specs=pl.BlockSpec((1, rows, tn), lambda l, j: (l, 0, j)),
        out_shape=jax.ShapeDtypeStruct((depth, rows, n), F32),
        compiler_params=_params("parallel", "parallel"),
        name="mod",
    )(c_rows, w_mod, b_mod.reshape(depth, 1, n))


FFN_CHUNK = 256


def _ffn_kernel(*refs, n_mix_a, sub):
    if n_mix_a:
        x_ref = refs[0]
        a_refs = refs[1:1 + n_mix_a]
        b_ref, mod_ref, gain_ref, wa_ref, wb_ref, win_ref, wout_ref, o_ref, act_ref = refs[1 + n_mix_a:]
    else:
        x_ref, mod_ref, gain_ref, win_ref, wout_ref, o_ref, act_ref = refs
    x = x_ref[0]
    if n_mix_a:
        if n_mix_a == 1:
            a = a_refs[0][0]
        else:
            a = (a_refs[0][0].astype(F32) + a_refs[1][0].astype(F32)).astype(BF)
        mix = _dot(a, wa_ref[...]) + _dot(b_ref[0], wb_ref[...])
        x = x + mod_ref[0, 5:6, :] * mix
    h = _norm_mod(x, gain_ref[sub:sub + 1, :], mod_ref[0, 3 * sub:3 * sub + 1, :],
                  mod_ref[0, 3 * sub + 1:3 * sub + 2, :]).astype(BF)
    d_ff = act_ref.shape[1]
    for c in range(d_ff // FFN_CHUNK):
        lo = c * FFN_CHUNK
        g = _dot(h, win_ref[:, lo:lo + FFN_CHUNK])
        u = _dot(h, win_ref[:, d_ff + lo:d_ff + lo + FFN_CHUNK])
        act_ref[:, lo:lo + FFN_CHUNK] = (g * _sigmoid(g) * u).astype(BF)
    y = _dot(act_ref[...], wout_ref[...])
    o_ref[0] = x + (0.5 * mod_ref[0, 3 * sub + 2:3 * sub + 3, :]) * y


def _ffn_call(x, mod, gain, w_in, w_out, layer, slot, sub, mix=None):
    bm, t, d = x.shape
    d_ff = w_out.shape[2]
    weight = lambda w: pl.BlockSpec((None, None) + w.shape[2:], lambda b, i: (layer, slot, 0, 0),
                                    pipeline_mode=pl.Buffered(1))
    tm = _row_tile(t, 512)
    row = lambda w: pl.BlockSpec((1, tm, w), lambda b, i: (b, i, 0))
    in_specs = [row(d)]
    args = [x]
    n_mix_a = 0
    if mix is not None:
        a_list, b_arr, w_a, w_b = mix
        n_mix_a = len(a_list)
        for a in a_list:
            in_specs.append(row(a.shape[-1]))
            args.append(a)
        in_specs.append(row(b_arr.shape[-1]))
        args.append(b_arr)
    in_specs += [pl.BlockSpec((1, 3 * N_SUB, d), lambda b, i: (b, 0, 0)), _const_spec(gain.shape)]
    args += [mod, gain]
    if mix is not None:
        in_specs += [_const_spec(w_a.shape), _const_spec(w_b.shape)]
        args += [w_a, w_b]
    in_specs += [weight(w_in), weight(w_out)]
    args += [w_in, w_out]
    return pl.pallas_call(
        functools.partial(_ffn_kernel, n_mix_a=n_mix_a, sub=sub),
        grid=(bm, t // tm),
        in_specs=in_specs,
        out_specs=row(d),
        out_shape=jax.ShapeDtypeStruct((bm, t, d), F32),
        scratch_shapes=[pltpu.VMEM((tm, d_ff), BF)],
        compiler_params=_params("parallel", "parallel"),
        name="ffn_mix" if mix is not None else "ffn",
    )(*args)


def _rope(x, cos, sin_signed):
    w = x.shape[1]
    n = w // LANES
    cos = jnp.concatenate([cos] * n, axis=1)
    sin_signed = jnp.concatenate([sin_signed] * n, axis=1)
    lane = lax.broadcasted_iota(jnp.int32, (1, w), 1)
    first_half = (lane & (HEAD_DIM - 1)) < (HEAD_DIM // 2)
    fwd = pltpu.roll(x, w - HEAD_DIM // 2, axis=1)
    bwd = pltpu.roll(x, HEAD_DIM // 2, axis=1)
    return x * cos + jnp.where(first_half, fwd, bwd) * sin_signed


def _even_in_kernel(*refs, has_rope, is_ctx):
    it = iter(refs)
    x_ref, mod_ref, gain_ref, w_ref, sgu_ref, ws_ref, bs_ref, qg_ref, kg_ref, bd_ref = (next(it) for _ in range(10))
    if has_rope:
        cos_ref, sin_ref = next(it), next(it)
    oa_ref, q_ref, kd_ref, vd_ref = (next(it) for _ in range(4))
    if is_ctx:
        k32_ref, v32_ref = next(it), next(it)

    tm = x_ref.shape[1]
    nck = tm // CHUNK
    x = x_ref[0]
    h = _norm_mod(x, gain_ref[1:2, :], mod_ref[0, 3:4, :], mod_ref[0, 4:5, :]).astype(BF)

    au = _gelu_tanh(_dot(h, w_ref[:, 0:A_WIDTH]))
    av = _gelu_tanh(_dot(h, w_ref[:, A_WIDTH:2 * A_WIDTH]))
    for g in range(A_GROUPS):
        cols = slice(g * A_DIM, (g + 1) * A_DIM)
        blk = av[:, cols]
        ms = jnp.mean(blk * blk, axis=-1, keepdims=True)
        vg = (blk * lax.rsqrt(ms + EPS) * sgu_ref[:, cols]).astype(BF)
        rhs = jnp.concatenate([vg[c * CHUNK:(c + 1) * CHUNK, :] for c in range(nck)], axis=1)
        s = _dot(ws_ref[g], rhs)
        for c in range(nck):
            rows = slice(c * CHUNK, (c + 1) * CHUNK)
            gate = s[:, c * A_DIM:(c + 1) * A_DIM] + bs_ref[g]
            oa_ref[0, rows, cols] = (au[rows, cols] * gate).astype(BF)

    lane = lax.broadcasted_iota(jnp.int32, (1, LANES), 1)
    lo_half = lane < HEAD_DIM
    if has_rope:
        cos = cos_ref[...]
        sin = sin_ref[...]

    off = 2 * A_WIDTH
    q = _dot(h, w_ref[:, off:off + B_WIDTH])
    ms = _dot((q * q).astype(BF), bd_ref[...])
    q = q * lax.rsqrt(ms + EPS) * qg_ref[...]
    if has_rope:
        q = _rope(q, cos, sin)
    q_ref[0] = (q * (HEAD_DIM ** -0.5)).astype(BF)

    off += B_WIDTH
    kvw = 2 * B_KV * HEAD_DIM
    k = _dot(h, w_ref[:, off:off + kvw])
    ms = _dot((k * k).astype(BF), bd_ref[0:kvw, 0:kvw])
    k = k * lax.rsqrt(ms + EPS) * kg_ref[...]
    if is_ctx:
        k32_ref[0] = jnp.where(lo_half, k[:, 0:LANES], k[:, LANES:2 * LANES])
    if has_rope:
        k = _rope(k, cos, sin)
    kd_ref[0] = k.astype(BF)
    off += kvw
    v = _dot(h, w_ref[:, off:off + kvw])
    if is_ctx:
        v32_ref[0] = jnp.where(lo_half, v[:, 0:LANES], v[:, LANES:2 * LANES])
    vd_ref[0] = v.astype(BF)


def _even_in_call(x, mod, gain, w, sgu, ws, bs, qg, kg, bd, rope, is_ctx):
    bm, t, d = x.shape
    tm = _row_tile(t, 512)
    kvw = 2 * B_KV * HEAD_DIM
    row = lambda w_: pl.BlockSpec((1, tm, w_), lambda b, i: (b, i, 0))
    in_specs = [row(d), pl.BlockSpec((1, 3 * N_SUB, d), lambda b, i: (b, 0, 0))]
    in_specs += [_const_spec(a.shape) for a in (gain, w, sgu, ws, bs, qg, kg, bd)]
    args = [x, mod, gain, w, sgu, ws, bs, qg, kg, bd]
    if rope is not None:
        in_specs += [pl.BlockSpec((tm, LANES), lambda b, i: (i, 0))] * 2
        args += list(rope)
    out_specs = [row(A_WIDTH), row(B_WIDTH), row(kvw), row(kvw)]
    out_shape = [jax.ShapeDtypeStruct((bm, t, A_WIDTH), BF), jax.ShapeDtypeStruct((bm, t, B_WIDTH), BF),
                 jax.ShapeDtypeStruct((bm, t, kvw), BF), jax.ShapeDtypeStruct((bm, t, kvw), BF)]
    if is_ctx:
        out_specs += [row(B_KV * HEAD_DIM)] * 2
        out_shape += [jax.ShapeDtypeStruct((bm, t, B_KV * HEAD_DIM), F32)] * 2
    return pl.pallas_call(
        functools.partial(_even_in_kernel, has_rope=rope is not None, is_ctx=is_ctx),
        grid=(bm, t // tm),
        in_specs=in_specs,
        out_specs=out_specs,
        out_shape=out_shape,
        compiler_params=_params("parallel", "parallel"),
        name="even_in",
    )(*args)


ATTN_ROWS = 128


def _attn_kernel(*refs, n_kv):
    q_ref = refs[0]
    k_refs = refs[1:1 + n_kv]
    v_refs = refs[1 + n_kv:1 + 2 * n_kv]
    o_ref = refs[1 + 2 * n_kv]
    if n_kv > 1:
        k_all, v_all = refs[2 + 2 * n_kv:]

        @pl.when(pl.program_id(1) == 0)
        def _():
            off = 0
            for k_ref, v_ref in zip(k_refs, v_refs):
                n = k_ref.shape[1]
                k_all[off:off + n, :] = k_ref[0]
                v_all[off:off + n, :] = v_ref[0]
                off += n

        keys = lambda cols: k_all[:, cols]
        vals = lambda cols: v_all[:, cols]
    else:
        keys = lambda cols: k_refs[0][0, :, cols]
        vals = lambda cols: v_refs[0][0, :, cols]
    tq = q_ref.shape[1]
    lane = lax.broadcasted_iota(jnp.int32, (1, LANES), 1)
    lo_half = lane < HEAD_DIM
    heads_per_kv = B_HEADS // B_KV
    units = [(r, h) for r in range(tq // ATTN_ROWS) for h in range(B_HEADS)]

    def kv_cols(h):
        kv = h // heads_per_kv
        return slice(kv * LANES, (kv + 1) * LANES)

    def scores(unit):
        r, h = unit
        p = h // 2
        qp = q_ref[0, r * ATTN_ROWS:(r + 1) * ATTN_ROWS, p * LANES:(p + 1) * LANES]
        keep = lo_half if h % 2 == 0 else jnp.logical_not(lo_half)
        return _dot_nt(jnp.where(keep, qp, jnp.zeros_like(qp)), keys(kv_cols(h)))

    def weights(s):
        e = jnp.exp(s - jnp.max(s, axis=-1, keepdims=True))
        return e.astype(BF), jnp.sum(e, axis=-1, keepdims=True)

    def attend(e, l, unit):
        return _dot(e, vals(kv_cols(unit[1]))) / l

    outs = {}
    s_next = scores(units[0])
    pending = None
    for i, unit in enumerate(units):
        s = s_next
        if i + 1 < len(units):
            s_next = scores(units[i + 1])
        e, l = weights(s)
        if pending is not None:
            outs[pending[2]] = attend(*pending)
        pending = (e, l, unit)
    outs[pending[2]] = attend(*pending)
    for r in range(tq // ATTN_ROWS):
        for p in range(B_WIDTH // LANES):
            o_ref[0, r * ATTN_ROWS:(r + 1) * ATTN_ROWS, p * LANES:(p + 1) * LANES] = jnp.where(
                lo_half, outs[(r, 2 * p)], outs[(r, 2 * p + 1)]).astype(BF)


def _attn_call(q, k_blocks, v_blocks):
    b, t, w = q.shape
    tq = _row_tile(t, 512)
    whole = lambda a: pl.BlockSpec((1,) + a.shape[1:], lambda i, j: (i, 0, 0))
    n_kv = len(k_blocks)
    scratch = []
    if n_kv > 1:
        s_all = sum(a.shape[1] for a in k_blocks)
        scratch = [pltpu.VMEM((s_all, k_blocks[0].shape[2]), BF), pltpu.VMEM((s_all, v_blocks[0].shape[2]), BF)]
    return pl.pallas_call(
        functools.partial(_attn_kernel, n_kv=n_kv),
        grid=(b, t // tq),
        in_specs=[pl.BlockSpec((1, tq, w), lambda i, j: (i, j, 0))]
        + [whole(a) for a in k_blocks] + [whole(a) for a in v_blocks],
        out_specs=pl.BlockSpec((1, tq, w), lambda i, j: (i, j, 0)),
        out_shape=jax.ShapeDtypeStruct((b, t, w), BF),
        scratch_shapes=scratch,
        compiler_params=_params("parallel", "arbitrary"),
        name="attn",
    )(q, *k_blocks, *v_blocks)


N_GATE_ROWS = 2 * N_STREAMS


def _log_sigmoid(x):
    return jnp.minimum(x, 0.0) - jnp.log(1.0 + jnp.exp(-jnp.abs(x)))


def _odd_in_kernel(x_ref, mod_ref, gain_ref, w_ref, wt_ref, wgt_ref, bg_ref,
                   q_ref, k_ref, d_ref, vt_ref, ot_ref, gt_ref):
    tm = x_ref.shape[1]
    x = x_ref[0]
    h = _norm_mod(x, gain_ref[1:2, :], mod_ref[0, 3:4, :], mod_ref[0, 4:5, :]).astype(BF)
    w = C_WIDTH
    q_ref[0] = _dot(h, w_ref[:, 0:w]).astype(BF)
    k_ref[0] = (_dot(h, w_ref[:, w:2 * w]) * (C_DK ** -0.5)).astype(BF)
    d_ref[0] = _dot(h, w_ref[:, 2 * w:2 * w + D_WIDTH]).astype(BF)
    vt = _dot_nt(wt_ref[0:w, :], h)
    ot = _sigmoid(_dot_nt(wt_ref[w:2 * w, :], h))
    gt = _dot_nt(wgt_ref[...], h)
    row = lax.broadcasted_iota(jnp.int32, (N_GATE_ROWS, CHUNK), 0)
    for c in range(tm // CHUNK):
        lanes = slice(c * CHUNK, (c + 1) * CHUNK)
        vt_ref[0, c] = vt[:, lanes].astype(BF)
        ot_ref[0, c] = ot[:, lanes].astype(BF)
        g = gt[:, lanes] + bg_ref[...]
        gt_ref[0, c] = jnp.where(row >= N_STREAMS, _log_sigmoid(g), g)


def _odd_in_call(x, mod, gain, w, wt, wgt, bg):
    bm, t, d = x.shape
    tm = _row_tile(t, 512)
    nck = tm // CHUNK
    row = lambda w_: pl.BlockSpec((1, tm, w_), lambda b, i: (b, i, 0))
    chunked = lambda r: pl.BlockSpec((1, nck, r, CHUNK), lambda b, i: (b, i, 0, 0))
    return pl.pallas_call(
        _odd_in_kernel,
        grid=(bm, t // tm),
        in_specs=[row(d), pl.BlockSpec((1, 3 * N_SUB, d), lambda b, i: (b, 0, 0))]
        + [_const_spec(a.shape) for a in (gain, w, wt, wgt, bg)],
        out_specs=[row(C_WIDTH), row(C_WIDTH), row(D_WIDTH), chunked(C_WIDTH), chunked(C_WIDTH),
                   chunked(N_GATE_ROWS)],
        out_shape=[jax.ShapeDtypeStruct((bm, t, C_WIDTH), BF), jax.ShapeDtypeStruct((bm, t, C_WIDTH), BF),
                   jax.ShapeDtypeStruct((bm, t, D_WIDTH), BF),
                   jax.ShapeDtypeStruct((bm, t // CHUNK, C_WIDTH, CHUNK), BF),
                   jax.ShapeDtypeStruct((bm, t // CHUNK, C_WIDTH, CHUNK), BF),
                   jax.ShapeDtypeStruct((bm, t // CHUNK, N_GATE_ROWS, CHUNK), F32)],
        compiler_params=_params("parallel", "parallel"),
        name="odd_in",
    )(x, mod, gain, w, wt, wgt, bg)


N_REP = 8
CN_ROWS = C_DV + N_REP


def _split3(x):
    hi = x.astype(BF)
    r1 = x - hi.astype(F32)
    mid = r1.astype(BF)
    lo = (r1 - mid.astype(F32)).astype(BF)
    return hi, mid, lo


def _mlstm_kernel(qf_ref, kf_ref, vtf_ref, otf_ref, gtf_ref, qb_ref, kb_ref, vtb_ref, otb_ref, gtb_ref,
                  c0_ref, n0_ref, m0_ref,
                  hf_ref, hb_ref, c1_ref, n1_ref, m1_ref,
                  cn_s, m_s):
    i = pl.program_id(1)
    nck = qf_ref.shape[1] // CHUNK

    @pl.when(i == 0)
    def _():
        for sl in range(N_STREAMS):
            cn_s[sl, 0:C_DV, :] = c0_ref[0, sl]
            cn_s[sl, C_DV:CN_ROWS, :] = jnp.broadcast_to(n0_ref[0, sl:sl + 1, :], (N_REP, C_DK))
        m_s[...] = m0_ref[0]

    r_i = lax.broadcasted_iota(jnp.int32, (CHUNK, CHUNK), 0)
    c_i = lax.broadcasted_iota(jnp.int32, (CHUNK, CHUNK), 1)
    upper = c_i >= r_i
    lower = c_i <= r_i
    tri_up = jnp.where(upper, 1.0, 0.0).astype(BF)
    tri_lo = jnp.where(lower, 1.0, 0.0).astype(BF)
    eye = jnp.where(c_i == r_i, 1.0, 0.0).astype(BF)
    ones = jnp.ones((CHUNK, CHUNK), BF)
    n_rows = nck * N_STREAMS
    assert n_rows <= CHUNK
    is_fwd = (lax.broadcasted_iota(jnp.int32, (n_rows, CHUNK), 0) & (N_STREAMS - 1)) < C_HEADS
    is_fwd8 = is_fwd[0:N_STREAMS]
    lane = lax.broadcasted_iota(jnp.int32, (n_rows, CHUNK), 1)

    g_i, g_f = [], []
    for c in range(nck):
        gf = gtf_ref[0, c]
        gb = gtb_ref[0, nck - 1 - c]
        g_i.append(jnp.where(is_fwd8, gf[0:N_STREAMS], gb[0:N_STREAMS]))
        g_f.append(jnp.where(is_fwd8, gf[N_STREAMS:N_GATE_ROWS], gb[N_STREAMS:N_GATE_ROWS]))
    g_i = jnp.concatenate(g_i, axis=0)
    g_f = jnp.concatenate(g_f, axis=0)
    parts = _split3(g_f)
    cum = jnp.where(is_fwd, sum(_dot(p, tri_up) for p in parts),
                    sum(_dot(p, tri_lo) for p in parts))
    total = sum(_dot(p, ones) for p in parts)
    r = g_i - cum
    run = r
    sh = 1
    while sh < CHUNK:
        prev = jnp.where(lane >= sh, pltpu.roll(run, sh, axis=1), -jnp.inf)
        nxt = jnp.where(lane < CHUNK - sh, pltpu.roll(run, CHUNK - sh, axis=1), -jnp.inf)
        run = jnp.maximum(run, jnp.where(is_fwd, prev, nxt))
        sh *= 2
    log_s = total - cum + g_i
    ls_max = jnp.max(log_s, axis=-1, keepdims=True)
    m = m_s[...]
    m_old, m_new = [], []
    for c in range(nck):
        rows = slice(c * N_STREAMS, (c + 1) * N_STREAMS)
        m_old.append(m)
        m = jnp.maximum(total[rows] + m, ls_max[rows])
        m_new.append(m)
    m_s[...] = m
    m_old = jnp.concatenate(m_old, axis=0)
    m_new = jnp.concatenate(m_new, axis=0)
    log_inter = cum + m_old
    m_t = jnp.maximum(log_inter, cum + run)
    bm = cum - m_t
    w_prev = jnp.exp(log_inter - m_t)
    e_inv = jnp.exp(-m_t)
    w_src = jnp.exp(log_s - m_new)
    w_keep = jnp.exp(total + m_old - m_new)
    r_cols = jnp.concatenate([r, jnp.zeros((CHUNK - n_rows, CHUNK), F32)], axis=0).T

    row = lambda a, j: a[j:j + 1, :]
    units = []
    for c in range(nck):
        for sl in range(N_STREAMS):
            fwd = sl < C_HEADS
            q_ref, k_ref, vt_ref, ot_ref, h_ref = ((qf_ref, kf_ref, vtf_ref, otf_ref, hf_ref) if fwd
                                                   else (qb_ref, kb_ref, vtb_ref, otb_ref, hb_ref))
            ck = c if fwd else nck - 1 - c
            rows = slice(ck * CHUNK, (ck + 1) * CHUNK)
            cols = slice((sl % C_HEADS) * C_DK, (sl % C_HEADS + 1) * C_DK)
            units.append((c * N_STREAMS + sl, sl, q_ref[0, rows, cols], k_ref[0, rows, cols],
                          vt_ref[0, ck, cols, :], ot_ref[0, ck, cols, :], h_ref, rows, cols))
    s_t = [_dot_nt(k, q) for _, _, q, k, _, _, _, _, _ in units]
    upd = []
    for j, _, _, k, vt, _, _, _, _ in units:
        ws = row(w_src, j)
        lhs = jnp.concatenate([(vt.astype(F32) * ws).astype(BF),
                               jnp.broadcast_to(ws, (N_REP, CHUNK)).astype(BF)], axis=0)
        upd.append(_dot(lhs, k))
    a_t = []
    for (j, sl, *_), s in zip(units, s_t):
        mask = upper if sl < C_HEADS else lower
        a_t.append(jnp.exp(jnp.where(mask, r_cols[:, j:j + 1] + row(bm, j), -jnp.inf)) * s)
    num = [_dot(u[4], a.astype(BF)) for u, a in zip(units, a_t)]
    cn = [cn_s[sl] for sl in range(N_STREAMS)]
    step_units = lambda c: range(c * N_STREAMS, (c + 1) * N_STREAMS)

    def read_state(c):
        out = []
        for idx in step_units(c):
            j, sl, q = units[idx][0:3]
            out.append(_dot_nt(cn[sl].astype(BF), q))
            cn[sl] = row(w_keep, j) * cn[sl] + upd[idx]
        return out

    inter = read_state(0)
    for c in range(nck):
        h_t = []
        for idx, it in zip(step_units(c), inter):
            j, ot = units[idx][0], units[idx][5]
            wp = row(w_prev, j)
            den = jnp.sum(a_t[idx], axis=0, keepdims=True) + wp * it[C_DV:C_DV + 1]
            denom = jnp.maximum(jnp.abs(den), row(e_inv, j))
            h_t.append(((num[idx] + wp * it[0:C_DV]) * (1.0 / denom) * ot.astype(F32)).astype(BF))
        if c + 1 < nck:
            inter = read_state(c + 1)
        h_rows = [_dot_nt(eye, h).astype(BF) for h in h_t]
        for idx, h in zip(step_units(c), h_rows):
            h_ref, rows, cols = units[idx][6:9]
            h_ref[0, rows, cols] = h
    for sl in range(N_STREAMS):
        cn_s[sl] = cn[sl]

    @pl.when(i == pl.num_programs(1) - 1)
    def _():
        for sl in range(N_STREAMS):
            c1_ref[0, sl] = cn_s[sl, 0:C_DV, :]
            n1_ref[0, sl:sl + 1, :] = cn_s[sl, C_DV:C_DV + 1, :]
        m1_ref[0] = m_s[...]


def _mlstm_call(q, k, vt, ot, gt, c0, n0, m0):
    b, t, w = q.shape
    tb = _row_tile(t, 512)
    nt = t // tb
    nck = tb // CHUNK
    fwd = pl.BlockSpec((1, tb, w), lambda i, j: (i, j, 0))
    bwd = pl.BlockSpec((1, tb, w), lambda i, j: (i, nt - 1 - j, 0))
    fwd_t = lambda r: pl.BlockSpec((1, nck, r, CHUNK), lambda i, j: (i, j, 0, 0))
    bwd_t = lambda r: pl.BlockSpec((1, nck, r, CHUNK), lambda i, j: (i, nt - 1 - j, 0, 0))
    st_c = pl.BlockSpec((1, N_STREAMS, C_DV, C_DK), lambda i, j: (i, 0, 0, 0))
    st_n = pl.BlockSpec((1, N_STREAMS, C_DK), lambda i, j: (i, 0, 0))
    st_m = pl.BlockSpec((1, N_STREAMS, CHUNK), lambda i, j: (i, 0, 0))
    return pl.pallas_call(
        _mlstm_kernel,
        grid=(b, nt),
        in_specs=[fwd, fwd, fwd_t(w), fwd_t(w), fwd_t(N_GATE_ROWS), bwd, bwd, bwd_t(w), bwd_t(w),
                  bwd_t(N_GATE_ROWS), st_c, st_n, st_m],
        out_specs=[fwd, bwd, st_c, st_n, st_m],
        out_shape=[jax.ShapeDtypeStruct((b, t, w), BF), jax.ShapeDtypeStruct((b, t, w), BF),
                   jax.ShapeDtypeStruct(c0.shape, F32), jax.ShapeDtypeStruct(n0.shape, F32),
                   jax.ShapeDtypeStruct(m0.shape, F32)],
        scratch_shapes=[pltpu.VMEM((N_STREAMS, CN_ROWS, C_DK), F32), pltpu.VMEM((N_STREAMS, CHUNK), F32)],
        compiler_params=_params("parallel", "arbitrary"),
        name="mlstm",
    )(q, k, vt, ot, gt, q, k, vt, ot, gt, c0, n0, m0)


def _fourier_kernel(d_ref, dft_t_ref, dft_c_ref, o_ref, y_s, *, scale):
    t = d_ref.shape[1]

    @pl.when(pl.program_id(1) == 0)
    def _():
        step = min(t, 512)
        for r in range(t // step):
            rows = slice(r * step, (r + 1) * step)
            for g in range(D_GROUPS):
                cols = slice(g * D_DIM, (g + 1) * D_DIM)
                y = _dot(d_ref[0, rows, cols], dft_c_ref[...])
                y_s[rows, cols] = y[:, 0:D_DIM].astype(BF)
                y_s[t + r * step:t + (r + 1) * step, cols] = y[:, D_DIM:2 * D_DIM].astype(BF)

    o_ref[0] = (_dot(dft_t_ref[...], y_s[...]) * scale).astype(BF)


def _fourier_call(d, dft_t, dft_c):
    b, t, w = d.shape
    tr = _row_tile(t, 512)
    return pl.pallas_call(
        functools.partial(_fourier_kernel, scale=float((t * D_DIM) ** -0.5)),
        grid=(b, t // tr),
        in_specs=[
            pl.BlockSpec((1, t, w), lambda i, j: (i, 0, 0)),
            pl.BlockSpec((tr, 2 * t), lambda i, j: (j, 0)),
            _const_spec(dft_c.shape),
        ],
        out_specs=pl.BlockSpec((1, tr, w), lambda i, j: (i, j, 0)),
        out_shape=jax.ShapeDtypeStruct((b, t, w), BF),
        scratch_shapes=[pltpu.VMEM((2 * t, w), BF)],
        compiler_params=_params("parallel", "arbitrary"),
        name="fourier",
    )(d, dft_t, dft_c)


DFT_COLS = 16
DFT_RESIDUES = 8


def _dft_rows_kernel(x_ref, f_ref, zr_ref, zi_ref):
    n1, nc, w = x_ref.shape[1:]
    z = _dot(f_ref[...], x_ref[0].reshape(n1 * nc, w))
    zr_ref[0] = z[0:n1 * nc].astype(BF).reshape(n1, nc, w)
    zi_ref[0] = z[n1 * nc:2 * n1 * nc].astype(BF).reshape(n1, nc, w)


def _dft_rows_call(x4, f1):
    b, n1, n2, w = x4.shape
    spec = pl.BlockSpec((1, n1, DFT_COLS, w), lambda i, j: (i, 0, j, 0))
    return pl.pallas_call(
        _dft_rows_kernel,
        grid=(b, n2 // DFT_COLS),
        in_specs=[spec, _const_spec(f1.shape)],
        out_specs=[spec, spec],
        out_shape=[jax.ShapeDtypeStruct(x4.shape, BF)] * 2,
        compiler_params=_params("parallel", "parallel"),
        name="dft_rows",
    )(x4, f1)


def _dft_cols_kernel(zr_ref, zi_ref, g_ref, c_ref, o_ref, *, scale):
    n2 = GRID_W
    nb = zr_ref.shape[1] // n2
    w = zr_ref.shape[2]
    xr, xi = [], []
    for r in range(nb):
        rows = slice(r * n2, (r + 1) * n2)
        z = jnp.concatenate([zr_ref[0, rows, :], zi_ref[0, rows, :]], axis=0)
        x = _dot(g_ref[r], z)
        xr.append(x[0:n2])
        xi.append(x[n2:2 * n2])
    xr = jnp.concatenate(xr, axis=0).astype(BF)
    xi = jnp.concatenate(xi, axis=0).astype(BF)
    for g in range(D_GROUPS):
        cols = slice(g * D_DIM, (g + 1) * D_DIM)
        y = _dot(jnp.concatenate([xr[:, cols], xi[:, cols]], axis=1), c_ref[...]) * scale
        for r in range(nb):
            o_ref[0, :, r * w + g * D_DIM:r * w + (g + 1) * D_DIM] = y[r * n2:(r + 1) * n2].astype(BF)


def _dft_cols_call(zr, zi, gmat, cmat, scale):
    b, t, w = zr.shape
    n2 = GRID_W
    n1 = t // n2
    nb = DFT_RESIDUES
    zspec = pl.BlockSpec((1, nb * n2, w), lambda i, j: (i, j, 0))
    return pl.pallas_call(
        functools.partial(_dft_cols_kernel, scale=scale),
        grid=(b, n1 // nb),
        in_specs=[zspec, zspec, pl.BlockSpec((nb, 2 * n2, 2 * n2), lambda i, j: (j, 0, 0)),
                  _const_spec(cmat.shape)],
        out_specs=pl.BlockSpec((1, n2, nb * w), lambda i, j: (i, 0, j)),
        out_shape=jax.ShapeDtypeStruct((b, n2, n1 * w), BF),
        compiler_params=_params("parallel", "parallel"),
        name="dft_cols",
    )(zr, zi, gmat, cmat)


def _cos_sin(num, den):
    ang = (num % den).astype(F32) * (2.0 * math.pi / den)
    return jnp.cos(ang), jnp.sin(ang)


def _dft_cos_sin(n):
    k = jnp.arange(n, dtype=jnp.int32)
    return _cos_sin(k[:, None] * k[None, :], n)


DENSE_DFT_MAX = 1024


def _fourier(d):
    b, t, w = d.shape
    cc, sc = _dft_cos_sin(D_DIM)
    scale = float((t * D_DIM) ** -0.5)
    if t <= DENSE_DFT_MAX:
        ct, st = _dft_cos_sin(t)
        return _fourier_call(d, jnp.concatenate([ct, -st], axis=1).astype(BF),
                             jnp.concatenate([cc, sc], axis=1).astype(BF))
    n2 = GRID_W
    n1 = t // n2
    assert n1 * n2 == t and n1 % DFT_RESIDUES == 0 and n2 % DFT_COLS == 0
    a1 = jnp.arange(n1, dtype=jnp.int32)
    a2 = jnp.arange(n2, dtype=jnp.int32)
    c1, s1 = _cos_sin(a1[:, None] * a1[None, :], n1)
    eye_c = jnp.eye(DFT_COLS, dtype=F32)
    f1 = jnp.concatenate([jnp.kron(c1, eye_c), jnp.kron(-s1, eye_c)], axis=0).astype(BF)
    tp = a1[:, None, None] + n1 * a2[None, :, None]
    gc, gs = _cos_sin(tp * a2[None, None, :], t)
    gmat = jnp.concatenate([jnp.concatenate([gc, gs], axis=2),
                            jnp.concatenate([-gs, gc], axis=2)], axis=1).astype(BF)
    cmat = jnp.concatenate([cc, sc], axis=0).astype(BF)
    zr, zi = _dft_rows_call(d.reshape(b, n1, n2, w), f1)
    out = _dft_cols_call(zr.reshape(b, t, w), zi.reshape(b, t, w), gmat, cmat, scale)
    return out.reshape(b, t, w)


def _rope_tables(t):
    rows = t // GRID_W
    r = jnp.repeat(jnp.arange(rows), GRID_W).astype(F32)
    cidx = jnp.tile(jnp.arange(GRID_W), rows).astype(F32)
    n_freq = HEAD_DIM // 4
    inv = ROPE_THETA ** (-jnp.arange(n_freq, dtype=F32) / n_freq)
    ang = jnp.concatenate([r[:, None] * inv, cidx[:, None] * inv], axis=-1)
    ang = jnp.concatenate([ang] * (2 * LANES // HEAD_DIM), axis=-1)
    lane = jnp.arange(LANES)
    sign = jnp.where((lane % HEAD_DIM) < HEAD_DIM // 2, -1.0, 1.0).astype(F32)
    return jnp.cos(ang), jnp.sin(ang) * sign


def _dup_heads(a):
    parts = []
    for h in range(B_KV):
        blk = a[..., h * HEAD_DIM:(h + 1) * HEAD_DIM]
        parts += [blk, blk]
    return jnp.concatenate(parts, axis=-1)


def kernel(x_prompt, x_sample, c, cache_k, cache_v, state_C, state_n, state_m, c_ctx, norm_gain, w_mod, b_mod,
           ffn_w_in, ffn_w_out, w_in_even, w_out_even, spatial_w, spatial_b, sgu_gain, q_gain, k_gain,
           w_in_odd, b_gate_odd, w_out_odd):
    depth = norm_gain.shape[0]
    batch, seq, d_model = x_prompt.shape
    dec_batch, dec_seq, _ = x_sample.shape

    rows = 1 + dec_batch
    rows_pad = -(-rows // 8) * 8
    c_rows = jnp.concatenate([c_ctx[None, :], c, jnp.zeros((rows_pad - rows, d_model), F32)], axis=0)
    mod_all = _mod_call(c_rows, w_mod, b_mod)
    mod_ctx = mod_all[:, 0:1].reshape(depth, 1, 3 * N_SUB, d_model)
    mod_lat = mod_all[:, 1:rows].reshape(depth, dec_batch, 3 * N_SUB, d_model)

    ffn_in = ffn_w_in.astype(BF)
    ffn_out = ffn_w_out.astype(BF)

    kv = B_KV * HEAD_DIM
    lane = jnp.arange(LANES)
    bd = ((jnp.arange(B_WIDTH)[:, None] // HEAD_DIM) == (jnp.arange(B_WIDTH)[None, :] // HEAD_DIM))
    bd = jnp.where(bd, 1.0 / HEAD_DIM, 0.0).astype(BF)

    def even_weights(e):
        w = w_in_even[e]
        o = 2 * A_WIDTH + B_WIDTH
        w_cat = jnp.concatenate([w[:, :o], _dup_heads(w[:, o:o + kv]), _dup_heads(w[:, o + kv:o + 2 * kv])], axis=1)
        bs = jnp.broadcast_to(spatial_b[e][:, :, None], (A_GROUPS, CHUNK, A_DIM)).astype(F32)
        return dict(
            w=w_cat.astype(BF), sgu=sgu_gain[e].reshape(1, A_WIDTH), ws=spatial_w[e].astype(BF), bs=bs,
            qg=jnp.tile(q_gain[e], B_HEADS)[None, :], kg=jnp.tile(k_gain[e], 2 * B_KV)[None, :],
            wo_a=w_out_even[e][:A_WIDTH].astype(BF), wo_b=w_out_even[e][A_WIDTH:].astype(BF))

    def odd_weights(e):
        w = w_in_odd[e]
        o = 2 * C_HEADS * C_DK + 2 * C_WIDTH
        n_g = 2 * N_STREAMS
        qk = 2 * C_HEADS * C_DK
        w_rows = jnp.concatenate([w[:, :qk], w[:, o + n_g:]], axis=1)
        w_t = w[:, qk:o].T
        perm = jnp.array([d * 2 * C_HEADS + g * C_HEADS + h
                          for g in range(2) for d in range(N_DIR) for h in range(C_HEADS)])
        wg_t = w[:, o:o + n_g][:, perm].T
        bg = jnp.broadcast_to(b_gate_odd[e].reshape(-1)[perm][:, None], (n_g, CHUNK)).astype(F32)
        return dict(w=w_rows.astype(BF), wt=w_t.astype(BF), wgt=wg_t.astype(BF), bg=bg,
                    wo_a=w_out_odd[e][:C_WIDTH].astype(BF), wo_b=w_out_odd[e][C_WIDTH:].astype(BF))

    def run_pass(x, mods, b, t, rope, ctx_even, ctx_odd, is_ctx):
        bm = x.shape[0]
        per_seq = lambda a: a.reshape(b, t, a.shape[-1])
        flat = lambda a: a.reshape(bm, -1, a.shape[-1])
        new_even, new_odd = [], []
        for l in range(depth):
            e = l // 2
            mod, gain = mods[l], norm_gain[l]
            x = _ffn_call(x, mod, gain, ffn_in, ffn_out, l, 0, sub=0)
            if l % 2 == 0:
                p = even_weights(e)
                outs = _even_in_call(x, mod, gain, p["w"], p["sgu"], p["ws"], p["bs"], p["qg"], p["kg"], bd,
                                     rope, is_ctx)
                oa, q, kd, vd = outs[:4]
                k_blocks, v_blocks = [per_seq(kd)], [per_seq(vd)]
                if is_ctx:
                    new_even.append((outs[4], outs[5]))
                else:
                    ck, cv = ctx_even[e]
                    k_blocks.insert(0, _dup_heads(ck.reshape(b, -1, kv)).astype(BF))
                    v_blocks.insert(0, _dup_heads(cv.reshape(b, -1, kv)).astype(BF))
                ob = _attn_call(per_seq(q), k_blocks, v_blocks)
                mix = ([oa], flat(ob), p["wo_a"], p["wo_b"])
            else:
                p = odd_weights(e)
                q, k, dd, vt, ot, gt = _odd_in_call(x, mod, gain, p["w"], p["wt"], p["wgt"], p["bg"])
                per_seq_t = lambda a: a.reshape((b, t // CHUNK) + a.shape[2:])
                c0, n0, m0 = ctx_odd[e]
                hf, hb, c1, n1, m1 = _mlstm_call(per_seq(q), per_seq(k), per_seq_t(vt), per_seq_t(ot),
                                                 per_seq_t(gt), c0, n0, m0)
                new_odd.append((c1, n1, m1))
                od = _fourier(per_seq(dd))
                mix = ([flat(hf), flat(hb)], flat(od), p["wo_a"], p["wo_b"])
            x = _ffn_call(x, mod, gain, ffn_in, ffn_out, l, 1, sub=2, mix=mix)
        return x, new_even, new_odd

    n_even = (depth + 1) // 2
    n_odd = depth // 2

    zero_state = (jnp.zeros((batch, N_STREAMS, C_DV, C_DK), F32), jnp.zeros((batch, N_STREAMS, C_DK), F32),
                  jnp.zeros((batch, N_STREAMS, CHUNK), F32))
    y, new_even, new_odd = run_pass(x_prompt.reshape(1, batch * seq, d_model), list(mod_ctx), batch, seq,
                                    None, None, [zero_state] * n_odd, True)
    y_prompt = y.reshape(batch, seq, d_model)
    new_cache_k = jnp.stack([k.reshape(batch, seq, B_KV, HEAD_DIM) for k, _ in new_even], axis=1)
    new_cache_v = jnp.stack([v.reshape(batch, seq, B_KV, HEAD_DIM) for _, v in new_even], axis=1)
    new_state_c = jnp.stack([c1.reshape(batch, N_DIR, C_HEADS, C_DV, C_DK) for c1, _, _ in new_odd], axis=1)
    new_state_n = jnp.stack([n1.reshape(batch, N_DIR, C_HEADS, C_DK) for _, n1, _ in new_odd], axis=1)
    new_state_m = jnp.stack([m1[:, :, 0].reshape(batch, N_DIR, C_HEADS) for _, _, m1 in new_odd], axis=1)

    ctx_even = [(cache_k[:, e], cache_v[:, e]) for e in range(n_even)]
    ctx_odd = []
    for e in range(n_odd):
        m0 = jnp.broadcast_to(state_m[:, e].reshape(dec_batch, N_STREAMS, 1), (dec_batch, N_STREAMS, CHUNK))
        ctx_odd.append((state_C[:, e].reshape(dec_batch, N_STREAMS, C_DV, C_DK),
                        state_n[:, e].reshape(dec_batch, N_STREAMS, C_DK), m0))
    y_sample, _, _ = run_pass(x_sample, list(mod_lat), dec_batch, dec_seq, _rope_tables(dec_seq),
                              ctx_even, ctx_odd, False)

    return (y_prompt, y_sample, new_cache_k, new_cache_v, new_state_c, new_state_n, new_state_m)
```

```python
import functools
import math

import jax
import jax.numpy as jnp
from jax import lax
from jax.experimental import pallas as pl
from jax.experimental.pallas import tpu as pltpu

BF = jnp.bfloat16
F32 = jnp.float32

EPS = 1e-6
ROPE_THETA = 10000.0
GRID_W = 64
CHUNK = 128
N_SUB = 3
A_GROUPS = 4
A_DIM = 128
A_WIDTH = A_GROUPS * A_DIM
B_HEADS = 8
B_KV = 2
HEAD_DIM = 64
B_WIDTH = B_HEADS * HEAD_DIM
C_HEADS = 4
C_DK = 128
C_DV = 128
C_WIDTH = C_HEADS * C_DV
N_DIR = 2
N_STREAMS = N_DIR * C_HEADS
D_GROUPS = 4
D_DIM = 128
D_WIDTH = D_GROUPS * D_DIM
LANES = 128

V7X_VMEM_BYTES = 64 * 1024 * 1024
VMEM_LIMIT = V7X_VMEM_BYTES - 4 * 1024 * 1024
ROW_TILE = 512
FFN_ROW_TILE = 1024


def _params(*sem):
    return pltpu.CompilerParams(dimension_semantics=sem, vmem_limit_bytes=VMEM_LIMIT)


def _dot(a, b):
    return jnp.dot(a, b, preferred_element_type=F32)


def _dot_nt(a, b):
    return lax.dot_general(a, b, (((1,), (1,)), ((), ())), preferred_element_type=F32)


def _dot_tn(a, b):
    return lax.dot_general(a, b, (((0,), (0,)), ((), ())), preferred_element_type=F32)


def _sigmoid(x):
    return 1.0 / (1.0 + jnp.exp(-x))


def _gelu_tanh(x):
    return 0.5 * x * (1.0 + jnp.tanh(math.sqrt(2.0 / math.pi) * (x + 0.044715 * (x * x * x))))


def _norm_mod(x, gain, shift, scale):
    ms = jnp.mean(x * x, axis=-1, keepdims=True)
    y = x * lax.rsqrt(ms + EPS) * gain
    return y * (1.0 + scale) + shift


def _const_spec(shape):
    zeros = (0,) * len(shape)
    return pl.BlockSpec(shape, lambda *_: zeros, pipeline_mode=pl.Buffered(1))


def _row_tile(t, target):
    tm = min(t, target)
    assert t % tm == 0
    return tm


def _mod_kernel(c_ref, w_ref, b_ref, o_ref):
    c = c_ref[...]
    sc = (c * _sigmoid(c)).astype(BF)
    o_ref[0] = _dot(sc, w_ref[0].astype(BF)) + b_ref[0]


def _mod_call(c_rows, w_mod, b_mod):
    depth, d, n = w_mod.shape
    rows = c_rows.shape[0]
    tn = 1024
    assert n % tn == 0
    return pl.pallas_call(
        _mod_kernel,
        grid=(depth, n // tn),
        in_specs=[
            pl.BlockSpec((rows, d), lambda l, j: (0, 0)),
            pl.BlockSpec((1, d, tn), lambda l, j: (l, 0, j)),
            pl.BlockSpec((1, 1, tn), lambda l, j: (l, 0, j)),
        ],
        out_specs=pl.BlockSpec((1, rows, tn), lambda l, j: (l, 0, j)),
        out_shape=jax.ShapeDtypeStruct((depth, rows, n), F32),
        compiler_params=_params("parallel", "parallel"),
        name="mod",
    )(c_rows, w_mod, b_mod.reshape(depth, 1, n))


FFN_CHUNK = 256


def _ffn_kernel(*refs, n_mix_a, sub):
    if n_mix_a:
        x_ref = refs[0]
        a_refs = refs[1:1 + n_mix_a]
        b_ref, mod_ref, gain_ref, wa_ref, wb_ref, win_ref, wout_ref, o_ref, act_ref = refs[1 + n_mix_a:]
    else:
        x_ref, mod_ref, gain_ref, win_ref, wout_ref, o_ref, act_ref = refs
    x = x_ref[0]
    if n_mix_a:
        if n_mix_a == 1:
            a = a_refs[0][0]
        else:
            a = (a_refs[0][0].astype(F32) + a_refs[1][0].astype(F32)).astype(BF)
        mix = _dot(a, wa_ref[...]) + _dot(b_ref[0], wb_ref[...])
        x = x + mod_ref[0, 5:6, :] * mix
    h = _norm_mod(x, gain_ref[sub:sub + 1, :], mod_ref[0, 3 * sub:3 * sub + 1, :],
                  mod_ref[0, 3 * sub + 1:3 * sub + 2, :]).astype(BF)
    d_ff = act_ref.shape[1]
    for c in range(d_ff // FFN_CHUNK):
        lo = c * FFN_CHUNK
        g = _dot(h, win_ref[:, lo:lo + FFN_CHUNK])
        u = _dot(h, win_ref[:, d_ff + lo:d_ff + lo + FFN_CHUNK])
        act_ref[:, lo:lo + FFN_CHUNK] = (g * _sigmoid(g) * u).astype(BF)
    y = _dot(act_ref[...], wout_ref[...])
    o_ref[0] = x + (0.5 * mod_ref[0, 3 * sub + 2:3 * sub + 3, :]) * y


def _ffn_call(x, mod, gain, w_in, w_out, layer, slot, sub, mix=None):
    bm, t, d = x.shape
    d_ff = w_out.shape[2]
    weight = lambda w: pl.BlockSpec((None, None) + w.shape[2:], lambda b, i: (layer, slot, 0, 0),
                                    pipeline_mode=pl.Buffered(1))
    tm = _row_tile(t, FFN_ROW_TILE)
    row = lambda w: pl.BlockSpec((1, tm, w), lambda b, i: (b, i, 0))
    in_specs = [row(d)]
    args = [x]
    n_mix_a = 0
    if mix is not None:
        a_list, b_arr, w_a, w_b = mix
        n_mix_a = len(a_list)
        for a in a_list:
            in_specs.append(row(a.shape[-1]))
            args.append(a)
        in_specs.append(row(b_arr.shape[-1]))
        args.append(b_arr)
    in_specs += [pl.BlockSpec((1, 3 * N_SUB, d), lambda b, i: (b, 0, 0)), _const_spec(gain.shape)]
    args += [mod, gain]
    if mix is not None:
        in_specs += [_const_spec(w_a.shape), _const_spec(w_b.shape)]
        args += [w_a, w_b]
    in_specs += [weight(w_in), weight(w_out)]
    args += [w_in, w_out]
    return pl.pallas_call(
        functools.partial(_ffn_kernel, n_mix_a=n_mix_a, sub=sub),
        grid=(bm, t // tm),
        in_specs=in_specs,
        out_specs=row(d),
        out_shape=jax.ShapeDtypeStruct((bm, t, d), F32),
        scratch_shapes=[pltpu.VMEM((tm, d_ff), BF)],
        compiler_params=_params("parallel", "parallel"),
        name="ffn_mix" if mix is not None else "ffn",
    )(*args)


def _rope(x, cos, sin_signed):
    w = x.shape[1]
    n = w // LANES
    cos = jnp.concatenate([cos] * n, axis=1)
    sin_signed = jnp.concatenate([sin_signed] * n, axis=1)
    lane = lax.broadcasted_iota(jnp.int32, (1, w), 1)
    first_half = (lane & (HEAD_DIM - 1)) < (HEAD_DIM // 2)
    fwd = pltpu.roll(x, w - HEAD_DIM // 2, axis=1)
    bwd = pltpu.roll(x, HEAD_DIM // 2, axis=1)
    return x * cos + jnp.where(first_half, fwd, bwd) * sin_signed


def _even_in_kernel(*refs, has_rope, is_ctx):
    it = iter(refs)
    x_ref, mod_ref, gain_ref, w_ref, sgu_ref, ws_ref, bs_ref, qg_ref, kg_ref, bd_ref = (next(it) for _ in range(10))
    if has_rope:
        cos_ref, sin_ref = next(it), next(it)
    oa_ref, q_ref, kd_ref, vd_ref = (next(it) for _ in range(4))
    if is_ctx:
        k32_ref, v32_ref = next(it), next(it)

    tm = x_ref.shape[1]
    nck = tm // CHUNK
    x = x_ref[0]
    h = _norm_mod(x, gain_ref[1:2, :], mod_ref[0, 3:4, :], mod_ref[0, 4:5, :]).astype(BF)

    off_q = 2 * A_WIDTH
    kvw = 2 * B_KV * HEAD_DIM
    off_k = off_q + B_WIDTH
    off_v = off_k + kvw
    au = _dot(h, w_ref[:, 0:A_WIDTH])
    av = _dot(h, w_ref[:, A_WIDTH:2 * A_WIDTH])
    q = _dot(h, w_ref[:, off_q:off_q + B_WIDTH])
    k = _dot(h, w_ref[:, off_k:off_k + kvw])
    v = _dot(h, w_ref[:, off_v:off_v + kvw])
    q_ms = _dot((q * q).astype(BF), bd_ref[...])
    k_ms = _dot((k * k).astype(BF), bd_ref[0:kvw, 0:kvw])

    av = _gelu_tanh(av)
    gates = []
    for g in range(A_GROUPS):
        cols = slice(g * A_DIM, (g + 1) * A_DIM)
        blk = av[:, cols]
        ms = jnp.mean(blk * blk, axis=-1, keepdims=True)
        vg = (blk * lax.rsqrt(ms + EPS) * sgu_ref[:, cols]).astype(BF)
        rhs = jnp.concatenate([vg[c * CHUNK:(c + 1) * CHUNK, :] for c in range(nck)], axis=1)
        gates.append(_dot(ws_ref[g], rhs))

    lane = lax.broadcasted_iota(jnp.int32, (1, LANES), 1)
    lo_half = lane < HEAD_DIM
    if has_rope:
        cos = cos_ref[...]
        sin = sin_ref[...]
    q = q * lax.rsqrt(q_ms + EPS) * qg_ref[...]
    if has_rope:
        q = _rope(q, cos, sin)
    q_ref[0] = (q * (HEAD_DIM ** -0.5)).astype(BF)
    k = k * lax.rsqrt(k_ms + EPS) * kg_ref[...]
    if is_ctx:
        k32_ref[0] = jnp.where(lo_half, k[:, 0:LANES], k[:, LANES:2 * LANES])
        v32_ref[0] = jnp.where(lo_half, v[:, 0:LANES], v[:, LANES:2 * LANES])
    if has_rope:
        k = _rope(k, cos, sin)
    kd_ref[0] = k.astype(BF)
    vd_ref[0] = v.astype(BF)
    au = _gelu_tanh(au)
    for g in range(A_GROUPS):
        cols = slice(g * A_DIM, (g + 1) * A_DIM)
        for c in range(nck):
            rows = slice(c * CHUNK, (c + 1) * CHUNK)
            gate = gates[g][:, c * A_DIM:(c + 1) * A_DIM] + bs_ref[g]
            oa_ref[0, rows, cols] = (au[rows, cols] * gate).astype(BF)


def _even_in_call(x, mod, gain, w, sgu, ws, bs, qg, kg, bd, rope, is_ctx):
    bm, t, d = x.shape
    tm = _row_tile(t, ROW_TILE)
    kvw = 2 * B_KV * HEAD_DIM
    row = lambda w_: pl.BlockSpec((1, tm, w_), lambda b, i: (b, i, 0))
    in_specs = [row(d), pl.BlockSpec((1, 3 * N_SUB, d), lambda b, i: (b, 0, 0))]
    in_specs += [_const_spec(a.shape) for a in (gain, w, sgu, ws, bs, qg, kg, bd)]
    args = [x, mod, gain, w, sgu, ws, bs, qg, kg, bd]
    if rope is not None:
        in_specs += [pl.BlockSpec((tm, LANES), lambda b, i: (i, 0))] * 2
        args += list(rope)
    out_specs = [row(A_WIDTH), row(B_WIDTH), row(kvw), row(kvw)]
    out_shape = [jax.ShapeDtypeStruct((bm, t, A_WIDTH), BF), jax.ShapeDtypeStruct((bm, t, B_WIDTH), BF),
                 jax.ShapeDtypeStruct((bm, t, kvw), BF), jax.ShapeDtypeStruct((bm, t, kvw), BF)]
    if is_ctx:
        out_specs += [row(B_KV * HEAD_DIM)] * 2
        out_shape += [jax.ShapeDtypeStruct((bm, t, B_KV * HEAD_DIM), F32)] * 2
    return pl.pallas_call(
        functools.partial(_even_in_kernel, has_rope=rope is not None, is_ctx=is_ctx),
        grid=(bm, t // tm),
        in_specs=in_specs,
        out_specs=out_specs,
        out_shape=out_shape,
        compiler_params=_params("parallel", "parallel"),
        name="even_in",
    )(*args)


ATTN_ROWS = 128


def _attn_kernel(*refs, n_kv):
    q_ref = refs[0]
    k_refs = refs[1:1 + n_kv]
    v_refs = refs[1 + n_kv:1 + 2 * n_kv]
    o_ref = refs[1 + 2 * n_kv]
    if n_kv > 1:
        k_all, v_all = refs[2 + 2 * n_kv:]

        @pl.when(pl.program_id(1) == 0)
        def _():
            off = 0
            for k_ref, v_ref in zip(k_refs, v_refs):
                n = k_ref.shape[1]
                k_all[off:off + n, :] = k_ref[0]
                v_all[off:off + n, :] = v_ref[0]
                off += n

        keys = lambda cols: k_all[:, cols]
        vals = lambda cols: v_all[:, cols]
    else:
        keys = lambda cols: k_refs[0][0, :, cols]
        vals = lambda cols: v_refs[0][0, :, cols]
    tq = q_ref.shape[1]
    lane = lax.broadcasted_iota(jnp.int32, (1, LANES), 1)
    lo_half = lane < HEAD_DIM
    heads_per_kv = B_HEADS // B_KV
    units = [(r, h) for r in range(tq // ATTN_ROWS) for h in range(B_HEADS)]

    def kv_cols(h):
        kv = h // heads_per_kv
        return slice(kv * LANES, (kv + 1) * LANES)

    def scores(unit):
        r, h = unit
        p = h // 2
        qp = q_ref[0, r * ATTN_ROWS:(r + 1) * ATTN_ROWS, p * LANES:(p + 1) * LANES]
        keep = lo_half if h % 2 == 0 else jnp.logical_not(lo_half)
        return _dot_nt(jnp.where(keep, qp, jnp.zeros_like(qp)), keys(kv_cols(h)))

    def weights(s):
        e = jnp.exp(s - jnp.max(s, axis=-1, keepdims=True))
        return e.astype(BF), jnp.sum(e, axis=-1, keepdims=True)

    def attend(e, l, unit):
        return _dot(e, vals(kv_cols(unit[1]))) / l

    outs = {}
    s_next = scores(units[0])
    pending = None
    for i, unit in enumerate(units):
        s = s_next
        if i + 1 < len(units):
            s_next = scores(units[i + 1])
        e, l = weights(s)
        if pending is not None:
            outs[pending[2]] = attend(*pending)
        pending = (e, l, unit)
    outs[pending[2]] = attend(*pending)
    for r in range(tq // ATTN_ROWS):
        for p in range(B_WIDTH // LANES):
            o_ref[0, r * ATTN_ROWS:(r + 1) * ATTN_ROWS, p * LANES:(p + 1) * LANES] = jnp.where(
                lo_half, outs[(r, 2 * p)], outs[(r, 2 * p + 1)]).astype(BF)


def _attn_call(q, k_blocks, v_blocks):
    b, t, w = q.shape
    tq = _row_tile(t, ROW_TILE)
    whole = lambda a: pl.BlockSpec((1,) + a.shape[1:], lambda i, j: (i, 0, 0))
    n_kv = len(k_blocks)
    scratch = []
    if n_kv > 1:
        s_all = sum(a.shape[1] for a in k_blocks)
        scratch = [pltpu.VMEM((s_all, k_blocks[0].shape[2]), BF), pltpu.VMEM((s_all, v_blocks[0].shape[2]), BF)]
    return pl.pallas_call(
        functools.partial(_attn_kernel, n_kv=n_kv),
        grid=(b, t // tq),
        in_specs=[pl.BlockSpec((1, tq, w), lambda i, j: (i, j, 0))]
        + [whole(a) for a in k_blocks] + [whole(a) for a in v_blocks],
        out_specs=pl.BlockSpec((1, tq, w), lambda i, j: (i, j, 0)),
        out_shape=jax.ShapeDtypeStruct((b, t, w), BF),
        scratch_shapes=scratch,
        compiler_params=_params("parallel", "arbitrary"),
        name="attn",
    )(q, *k_blocks, *v_blocks)


N_GATE_ROWS = 2 * N_STREAMS


def _log_sigmoid(x):
    return jnp.minimum(x, 0.0) - jnp.log(1.0 + jnp.exp(-jnp.abs(x)))


def _odd_in_kernel(x_ref, mod_ref, gain_ref, w_ref, wt_ref, wgt_ref, bg_ref,
                   q_ref, k_ref, d_ref, vt_ref, ot_ref, gt_ref):
    tm = x_ref.shape[1]
    x = x_ref[0]
    h = _norm_mod(x, gain_ref[1:2, :], mod_ref[0, 3:4, :], mod_ref[0, 4:5, :]).astype(BF)
    w = C_WIDTH
    q_ref[0] = _dot(h, w_ref[:, 0:w]).astype(BF)
    k_ref[0] = (_dot(h, w_ref[:, w:2 * w]) * (C_DK ** -0.5)).astype(BF)
    d_ref[0] = _dot(h, w_ref[:, 2 * w:2 * w + D_WIDTH]).astype(BF)
    vt = _dot_nt(wt_ref[0:w, :], h)
    ot = _sigmoid(_dot_nt(wt_ref[w:2 * w, :], h))
    gt = _dot_nt(wgt_ref[...], h)
    row = lax.broadcasted_iota(jnp.int32, (N_GATE_ROWS, CHUNK), 0)
    for c in range(tm // CHUNK):
        lanes = slice(c * CHUNK, (c + 1) * CHUNK)
        vt_ref[0, c] = vt[:, lanes].astype(BF)
        ot_ref[0, c] = ot[:, lanes].astype(BF)
        g = gt[:, lanes] + bg_ref[...]
        gt_ref[0, c] = jnp.where(row >= N_STREAMS, _log_sigmoid(g), g)


def _odd_in_call(x, mod, gain, w, wt, wgt, bg):
    bm, t, d = x.shape
    tm = _row_tile(t, ROW_TILE)
    nck = tm // CHUNK
    row = lambda w_: pl.BlockSpec((1, tm, w_), lambda b, i: (b, i, 0))
    chunked = lambda r: pl.BlockSpec((1, nck, r, CHUNK), lambda b, i: (b, i, 0, 0))
    return pl.pallas_call(
        _odd_in_kernel,
        grid=(bm, t // tm),
        in_specs=[row(d), pl.BlockSpec((1, 3 * N_SUB, d), lambda b, i: (b, 0, 0))]
        + [_const_spec(a.shape) for a in (gain, w, wt, wgt, bg)],
        out_specs=[row(C_WIDTH), row(C_WIDTH), row(D_WIDTH), chunked(C_WIDTH), chunked(C_WIDTH),
                   chunked(N_GATE_ROWS)],
        out_shape=[jax.ShapeDtypeStruct((bm, t, C_WIDTH), BF), jax.ShapeDtypeStruct((bm, t, C_WIDTH), BF),
                   jax.ShapeDtypeStruct((bm, t, D_WIDTH), BF),
                   jax.ShapeDtypeStruct((bm, t // CHUNK, C_WIDTH, CHUNK), BF),
                   jax.ShapeDtypeStruct((bm, t // CHUNK, C_WIDTH, CHUNK), BF),
                   jax.ShapeDtypeStruct((bm, t // CHUNK, N_GATE_ROWS, CHUNK), F32)],
        compiler_params=_params("parallel", "parallel"),
        name="odd_in",
    )(x, mod, gain, w, wt, wgt, bg)


N_REP = 8
CN_ROWS = C_DV + N_REP


def _split3(x):
    hi = x.astype(BF)
    r1 = x - hi.astype(F32)
    mid = r1.astype(BF)
    lo = (r1 - mid.astype(F32)).astype(BF)
    return hi, mid, lo


def _mlstm_kernel(qf_ref, kf_ref, vtf_ref, otf_ref, gtf_ref, qb_ref, kb_ref, vtb_ref, otb_ref, gtb_ref,
                  c0_ref, n0_ref, m0_ref,
                  hf_ref, hb_ref, c1_ref, n1_ref, m1_ref,
                  cn_s, m_s):
    i = pl.program_id(1)
    nck = qf_ref.shape[1] // CHUNK

    @pl.when(i == 0)
    def _():
        for sl in range(N_STREAMS):
            cn_s[sl, 0:C_DV, :] = c0_ref[0, sl]
            cn_s[sl, C_DV:CN_ROWS, :] = jnp.broadcast_to(n0_ref[0, sl:sl + 1, :], (N_REP, C_DK))
        m_s[...] = m0_ref[0]

    r_i = lax.broadcasted_iota(jnp.int32, (CHUNK, CHUNK), 0)
    c_i = lax.broadcasted_iota(jnp.int32, (CHUNK, CHUNK), 1)
    upper = c_i >= r_i
    lower = c_i <= r_i
    tri_up = jnp.where(upper, 1.0, 0.0).astype(BF)
    tri_lo = jnp.where(lower, 1.0, 0.0).astype(BF)
    eye = jnp.where(c_i == r_i, 1.0, 0.0).astype(BF)
    ones = jnp.ones((CHUNK, CHUNK), BF)
    n_rows = nck * N_STREAMS
    assert n_rows <= CHUNK
    is_fwd = (lax.broadcasted_iota(jnp.int32, (n_rows, CHUNK), 0) & (N_STREAMS - 1)) < C_HEADS
    is_fwd8 = is_fwd[0:N_STREAMS]
    lane = lax.broadcasted_iota(jnp.int32, (n_rows, CHUNK), 1)

    g_i, g_f = [], []
    for c in range(nck):
        gf = gtf_ref[0, c]
        gb = gtb_ref[0, nck - 1 - c]
        g_i.append(jnp.where(is_fwd8, gf[0:N_STREAMS], gb[0:N_STREAMS]))
        g_f.append(jnp.where(is_fwd8, gf[N_STREAMS:N_GATE_ROWS], gb[N_STREAMS:N_GATE_ROWS]))
    g_i = jnp.concatenate(g_i, axis=0)
    g_f = jnp.concatenate(g_f, axis=0)
    parts = _split3(g_f)
    cum = jnp.where(is_fwd, sum(_dot(p, tri_up) for p in parts),
                    sum(_dot(p, tri_lo) for p in parts))
    total = sum(_dot(p, ones) for p in parts)
    r = g_i - cum
    run = r
    sh = 1
    while sh < CHUNK:
        prev = jnp.where(lane >= sh, pltpu.roll(run, sh, axis=1), -jnp.inf)
        nxt = jnp.where(lane < CHUNK - sh, pltpu.roll(run, CHUNK - sh, axis=1), -jnp.inf)
        run = jnp.maximum(run, jnp.where(is_fwd, prev, nxt))
        sh *= 2
    log_s = total - cum + g_i
    ls_max = jnp.max(log_s, axis=-1, keepdims=True)
    m = m_s[...]
    m_old, m_new = [], []
    for c in range(nck):
        rows = slice(c * N_STREAMS, (c + 1) * N_STREAMS)
        m_old.append(m)
        m = jnp.maximum(total[rows] + m, ls_max[rows])
        m_new.append(m)
    m_s[...] = m
    m_old = jnp.concatenate(m_old, axis=0)
    m_new = jnp.concatenate(m_new, axis=0)
    log_inter = cum + m_old
    m_t = jnp.maximum(log_inter, cum + run)
    bm = cum - m_t
    w_prev = jnp.exp(log_inter - m_t)
    e_inv = jnp.exp(-m_t)
    w_src = jnp.exp(log_s - m_new)
    w_keep = jnp.exp(total + m_old - m_new)
    r_cols = jnp.concatenate([r, jnp.zeros((CHUNK - n_rows, CHUNK), F32)], axis=0).T

    row = lambda a, j: a[j:j + 1, :]
    units = []
    for c in range(nck):
        for sl in range(N_STREAMS):
            fwd = sl < C_HEADS
            q_ref, k_ref, vt_ref, ot_ref, h_ref = ((qf_ref, kf_ref, vtf_ref, otf_ref, hf_ref) if fwd
                                                   else (qb_ref, kb_ref, vtb_ref, otb_ref, hb_ref))
            ck = c if fwd else nck - 1 - c
            rows = slice(ck * CHUNK, (ck + 1) * CHUNK)
            cols = slice((sl % C_HEADS) * C_DK, (sl % C_HEADS + 1) * C_DK)
            units.append((c * N_STREAMS + sl, sl, q_ref[0, rows, cols], k_ref[0, rows, cols],
                          vt_ref[0, ck, cols, :], ot_ref[0, ck, cols, :], h_ref, rows, cols))
    s_t = [_dot_nt(k, q) for _, _, q, k, _, _, _, _, _ in units]
    upd = []
    for j, _, _, k, vt, _, _, _, _ in units:
        ws = row(w_src, j)
        lhs = jnp.concatenate([(vt.astype(F32) * ws).astype(BF),
                               jnp.broadcast_to(ws, (N_REP, CHUNK)).astype(BF)], axis=0)
        upd.append(_dot(lhs, k))
    a_t = []
    for (j, sl, *_), s in zip(units, s_t):
        mask = upper if sl < C_HEADS else lower
        a_t.append(jnp.exp(jnp.where(mask, r_cols[:, j:j + 1] + row(bm, j), -jnp.inf)) * s)
    num = [_dot(u[4], a.astype(BF)) for u, a in zip(units, a_t)]
    cn = [cn_s[sl] for sl in range(N_STREAMS)]
    step_units = lambda c: range(c * N_STREAMS, (c + 1) * N_STREAMS)

    def read_state(c):
        out = []
        for idx in step_units(c):
            j, sl, q = units[idx][0:3]
            out.append(_dot_nt(cn[sl].astype(BF), q))
            cn[sl] = row(w_keep, j) * cn[sl] + upd[idx]
        return out

    inter = read_state(0)
    for c in range(nck):
        h_t = []
        for idx, it in zip(step_units(c), inter):
            j, ot = units[idx][0], units[idx][5]
            wp = row(w_prev, j)
            den = jnp.sum(a_t[idx], axis=0, keepdims=True) + wp * it[C_DV:C_DV + 1]
            denom = jnp.maximum(jnp.abs(den), row(e_inv, j))
            h_t.append(((num[idx] + wp * it[0:C_DV]) * (1.0 / denom) * ot.astype(F32)).astype(BF))
        if c + 1 < nck:
            inter = read_state(c + 1)
        h_rows = [_dot_nt(eye, h).astype(BF) for h in h_t]
        for idx, h in zip(step_units(c), h_rows):
            h_ref, rows, cols = units[idx][6:9]
            h_ref[0, rows, cols] = h
    for sl in range(N_STREAMS):
        cn_s[sl] = cn[sl]

    @pl.when(i == pl.num_programs(1) - 1)
    def _():
        for sl in range(N_STREAMS):
            c1_ref[0, sl] = cn_s[sl, 0:C_DV, :]
            n1_ref[0, sl:sl + 1, :] = cn_s[sl, C_DV:C_DV + 1, :]
        m1_ref[0] = m_s[...]


def _mlstm_call(q, k, vt, ot, gt, c0, n0, m0):
    b, t, w = q.shape
    tb = _row_tile(t, ROW_TILE)
    nt = t // tb
    nck = tb // CHUNK
    fwd = pl.BlockSpec((1, tb, w), lambda i, j: (i, j, 0))
    bwd = pl.BlockSpec((1, tb, w), lambda i, j: (i, nt - 1 - j, 0))
    fwd_t = lambda r: pl.BlockSpec((1, nck, r, CHUNK), lambda i, j: (i, j, 0, 0))
    bwd_t = lambda r: pl.BlockSpec((1, nck, r, CHUNK), lambda i, j: (i, nt - 1 - j, 0, 0))
    st_c = pl.BlockSpec((1, N_STREAMS, C_DV, C_DK), lambda i, j: (i, 0, 0, 0))
    st_n = pl.BlockSpec((1, N_STREAMS, C_DK), lambda i, j: (i, 0, 0))
    st_m = pl.BlockSpec((1, N_STREAMS, CHUNK), lambda i, j: (i, 0, 0))
    return pl.pallas_call(
        _mlstm_kernel,
        grid=(b, nt),
        in_specs=[fwd, fwd, fwd_t(w), fwd_t(w), fwd_t(N_GATE_ROWS), bwd, bwd, bwd_t(w), bwd_t(w),
                  bwd_t(N_GATE_ROWS), st_c, st_n, st_m],
        out_specs=[fwd, bwd, st_c, st_n, st_m],
        out_shape=[jax.ShapeDtypeStruct((b, t, w), BF), jax.ShapeDtypeStruct((b, t, w), BF),
                   jax.ShapeDtypeStruct(c0.shape, F32), jax.ShapeDtypeStruct(n0.shape, F32),
                   jax.ShapeDtypeStruct(m0.shape, F32)],
        scratch_shapes=[pltpu.VMEM((N_STREAMS, CN_ROWS, C_DK), F32), pltpu.VMEM((N_STREAMS, CHUNK), F32)],
        compiler_params=_params("parallel", "arbitrary"),
        name="mlstm",
    )(q, k, vt, ot, gt, q, k, vt, ot, gt, c0, n0, m0)


def _fourier_kernel(d_ref, dft_t_ref, dft_c_ref, o_ref, y_s, *, scale):
    t = d_ref.shape[1]

    @pl.when(pl.program_id(1) == 0)
    def _():
        step = min(t, 512)
        for r in range(t // step):
            rows = slice(r * step, (r + 1) * step)
            for g in range(D_GROUPS):
                cols = slice(g * D_DIM, (g + 1) * D_DIM)
                y = _dot(d_ref[0, rows, cols], dft_c_ref[...])
                y_s[rows, cols] = y[:, 0:D_DIM].astype(BF)
                y_s[t + r * step:t + (r + 1) * step, cols] = y[:, D_DIM:2 * D_DIM].astype(BF)

    o_ref[0] = (_dot(dft_t_ref[...], y_s[...]) * scale).astype(BF)


def _fourier_call(d, dft_t, dft_c):
    b, t, w = d.shape
    tr = _row_tile(t, ROW_TILE)
    return pl.pallas_call(
        functools.partial(_fourier_kernel, scale=float((t * D_DIM) ** -0.5)),
        grid=(b, t // tr),
        in_specs=[
            pl.BlockSpec((1, t, w), lambda i, j: (i, 0, 0)),
            pl.BlockSpec((tr, 2 * t), lambda i, j: (j, 0)),
            _const_spec(dft_c.shape),
        ],
        out_specs=pl.BlockSpec((1, tr, w), lambda i, j: (i, j, 0)),
        out_shape=jax.ShapeDtypeStruct((b, t, w), BF),
        scratch_shapes=[pltpu.VMEM((2 * t, w), BF)],
        compiler_params=_params("parallel", "arbitrary"),
        name="fourier",
    )(d, dft_t, dft_c)


DFT_COLS = 16
DFT_RESIDUES = 8


def _dft_rows_kernel(x_ref, f_ref, zr_ref, zi_ref):
    n1, nc, w = x_ref.shape[1:]
    z = _dot(f_ref[...], x_ref[0].reshape(n1 * nc, w))
    zr_ref[0] = z[0:n1 * nc].astype(BF).reshape(n1, nc, w)
    zi_ref[0] = z[n1 * nc:2 * n1 * nc].astype(BF).reshape(n1, nc, w)


def _dft_rows_call(x4, f1):
    b, n1, n2, w = x4.shape
    spec = pl.BlockSpec((1, n1, DFT_COLS, w), lambda i, j: (i, 0, j, 0))
    return pl.pallas_call(
        _dft_rows_kernel,
        grid=(b, n2 // DFT_COLS),
        in_specs=[spec, _const_spec(f1.shape)],
        out_specs=[spec, spec],
        out_shape=[jax.ShapeDtypeStruct(x4.shape, BF)] * 2,
        compiler_params=_params("parallel", "parallel"),
        name="dft_rows",
    )(x4, f1)


def _dft_cols_kernel(zr_ref, zi_ref, g_ref, c_ref, o_ref, *, scale):
    n2 = GRID_W
    nb = zr_ref.shape[1]
    w = zr_ref.shape[3]
    xr, xi = [], []
    for r in range(nb):
        z = jnp.concatenate([zr_ref[0, r], zi_ref[0, r]], axis=0)
        x = _dot(g_ref[r], z)
        xr.append(x[0:n2])
        xi.append(x[n2:2 * n2])
    xr = jnp.concatenate(xr, axis=0).astype(BF)
    xi = jnp.concatenate(xi, axis=0).astype(BF)
    for g in range(D_GROUPS):
        cols = slice(g * D_DIM, (g + 1) * D_DIM)
        y = _dot(jnp.concatenate([xr[:, cols], xi[:, cols]], axis=1), c_ref[...]) * scale
        for r in range(nb):
            o_ref[0, :, r * w + g * D_DIM:r * w + (g + 1) * D_DIM] = y[r * n2:(r + 1) * n2].astype(BF)


def _dft_cols_call(zr, zi, gmat, cmat, scale):
    b, n1, n2, w = zr.shape
    nb = DFT_RESIDUES
    zspec = pl.BlockSpec((1, nb, n2, w), lambda i, j: (i, j, 0, 0))
    return pl.pallas_call(
        functools.partial(_dft_cols_kernel, scale=scale),
        grid=(b, n1 // nb),
        in_specs=[zspec, zspec, pl.BlockSpec((nb, 2 * n2, 2 * n2), lambda i, j: (j, 0, 0)),
                  _const_spec(cmat.shape)],
        out_specs=pl.BlockSpec((1, n2, nb * w), lambda i, j: (i, 0, j)),
        out_shape=jax.ShapeDtypeStruct((b, n2, n1 * w), BF),
        compiler_params=_params("parallel", "parallel"),
        name="dft_cols",
    )(zr, zi, gmat, cmat)


def _cos_sin(num, den):
    ang = (num % den).astype(F32) * (2.0 * math.pi / den)
    return jnp.cos(ang), jnp.sin(ang)


def _dft_cos_sin(n):
    k = jnp.arange(n, dtype=jnp.int32)
    return _cos_sin(k[:, None] * k[None, :], n)


DENSE_DFT_MAX = 1024


def _fourier(d):
    b, t, w = d.shape
    cc, sc = _dft_cos_sin(D_DIM)
    scale = float((t * D_DIM) ** -0.5)
    if t <= DENSE_DFT_MAX:
        ct, st = _dft_cos_sin(t)
        return _fourier_call(d, jnp.concatenate([ct, -st], axis=1).astype(BF),
                             jnp.concatenate([cc, sc], axis=1).astype(BF))
    n2 = GRID_W
    n1 = t // n2
    assert n1 * n2 == t and n1 % DFT_RESIDUES == 0 and n2 % DFT_COLS == 0
    a1 = jnp.arange(n1, dtype=jnp.int32)
    a2 = jnp.arange(n2, dtype=jnp.int32)
    c1, s1 = _cos_sin(a1[:, None] * a1[None, :], n1)
    eye_c = jnp.eye(DFT_COLS, dtype=F32)
    f1 = jnp.concatenate([jnp.kron(c1, eye_c), jnp.kron(-s1, eye_c)], axis=0).astype(BF)
    tp = a1[:, None, None] + n1 * a2[None, :, None]
    gc, gs = _cos_sin(tp * a2[None, None, :], t)
    gmat = jnp.concatenate([jnp.concatenate([gc, gs], axis=2),
                            jnp.concatenate([-gs, gc], axis=2)], axis=1).astype(BF)
    cmat = jnp.concatenate([cc, sc], axis=0).astype(BF)
    zr, zi = _dft_rows_call(d.reshape(b, n1, n2, w), f1)
    out = _dft_cols_call(zr, zi, gmat, cmat, scale)
    return out.reshape(b, t, w)


def _rope_tables(t):
    rows = t // GRID_W
    r = jnp.repeat(jnp.arange(rows), GRID_W).astype(F32)
    cidx = jnp.tile(jnp.arange(GRID_W), rows).astype(F32)
    n_freq = HEAD_DIM // 4
    inv = ROPE_THETA ** (-jnp.arange(n_freq, dtype=F32) / n_freq)
    ang = jnp.concatenate([r[:, None] * inv, cidx[:, None] * inv], axis=-1)
    ang = jnp.concatenate([ang] * (2 * LANES // HEAD_DIM), axis=-1)
    lane = jnp.arange(LANES)
    sign = jnp.where((lane % HEAD_DIM) < HEAD_DIM // 2, -1.0, 1.0).astype(F32)
    return jnp.cos(ang), jnp.sin(ang) * sign


def _dup_heads(a):
    parts = []
    for h in range(B_KV):
        blk = a[..., h * HEAD_DIM:(h + 1) * HEAD_DIM]
        parts += [blk, blk]
    return jnp.concatenate(parts, axis=-1)


def kernel(x_prompt, x_sample, c, cache_k, cache_v, state_C, state_n, state_m, c_ctx, norm_gain, w_mod, b_mod,
           ffn_w_in, ffn_w_out, w_in_even, w_out_even, spatial_w, spatial_b, sgu_gain, q_gain, k_gain,
           w_in_odd, b_gate_odd, w_out_odd):
    depth = norm_gain.shape[0]
    batch, seq, d_model = x_prompt.shape
    dec_batch, dec_seq, _ = x_sample.shape

    rows = 1 + dec_batch
    rows_pad = -(-rows // 8) * 8
    c_rows = jnp.concatenate([c_ctx[None, :], c, jnp.zeros((rows_pad - rows, d_model), F32)], axis=0)
    mod_all = _mod_call(c_rows, w_mod, b_mod)
    mod_ctx = mod_all[:, 0:1].reshape(depth, 1, 3 * N_SUB, d_model)
    mod_lat = mod_all[:, 1:rows].reshape(depth, dec_batch, 3 * N_SUB, d_model)

    ffn_in = ffn_w_in.astype(BF)
    ffn_out = ffn_w_out.astype(BF)

    kv = B_KV * HEAD_DIM
    lane = jnp.arange(LANES)
    bd = ((jnp.arange(B_WIDTH)[:, None] // HEAD_DIM) == (jnp.arange(B_WIDTH)[None, :] // HEAD_DIM))
    bd = jnp.where(bd, 1.0 / HEAD_DIM, 0.0).astype(BF)

    def even_weights(e):
        w = w_in_even[e]
        o = 2 * A_WIDTH + B_WIDTH
        w_cat = jnp.concatenate([w[:, :o], _dup_heads(w[:, o:o + kv]), _dup_heads(w[:, o + kv:o + 2 * kv])], axis=1)
        bs = jnp.broadcast_to(spatial_b[e][:, :, None], (A_GROUPS, CHUNK, A_DIM)).astype(F32)
        return dict(
            w=w_cat.astype(BF), sgu=sgu_gain[e].reshape(1, A_WIDTH), ws=spatial_w[e].astype(BF), bs=bs,
            qg=jnp.tile(q_gain[e], B_HEADS)[None, :], kg=jnp.tile(k_gain[e], 2 * B_KV)[None, :],
            wo_a=w_out_even[e][:A_WIDTH].astype(BF), wo_b=w_out_even[e][A_WIDTH:].astype(BF))

    def odd_weights(e):
        w = w_in_odd[e]
        o = 2 * C_HEADS * C_DK + 2 * C_WIDTH
        n_g = 2 * N_STREAMS
        qk = 2 * C_HEADS * C_DK
        w_rows = jnp.concatenate([w[:, :qk], w[:, o + n_g:]], axis=1)
        w_t = w[:, qk:o].T
        perm = jnp.array([d * 2 * C_HEADS + g * C_HEADS + h
                          for g in range(2) for d in range(N_DIR) for h in range(C_HEADS)])
        wg_t = w[:, o:o + n_g][:, perm].T
        bg = jnp.broadcast_to(b_gate_odd[e].reshape(-1)[perm][:, None], (n_g, CHUNK)).astype(F32)
        return dict(w=w_rows.astype(BF), wt=w_t.astype(BF), wgt=wg_t.astype(BF), bg=bg,
                    wo_a=w_out_odd[e][:C_WIDTH].astype(BF), wo_b=w_out_odd[e][C_WIDTH:].astype(BF))

    def run_pass(x, mods, b, t, rope, ctx_even, ctx_odd, is_ctx):
        bm = x.shape[0]
        per_seq = lambda a: a.reshape(b, t, a.shape[-1])
        flat = lambda a: a.reshape(bm, -1, a.shape[-1])
        new_even, new_odd = [], []
        for l in range(depth):
            e = l // 2
            mod, gain = mods[l], norm_gain[l]
            x = _ffn_call(x, mod, gain, ffn_in, ffn_out, l, 0, sub=0)
            if l % 2 == 0:
                p = even_weights(e)
                outs = _even_in_call(x, mod, gain, p["w"], p["sgu"], p["ws"], p["bs"], p["qg"], p["kg"], bd,
                                     rope, is_ctx)
                oa, q, kd, vd = outs[:4]
                k_blocks, v_blocks = [per_seq(kd)], [per_seq(vd)]
                if is_ctx:
                    new_even.append((outs[4], outs[5]))
                else:
                    ck, cv = ctx_even[e]
                    k_blocks.insert(0, _dup_heads(ck.reshape(b, -1, kv)).astype(BF))
                    v_blocks.insert(0, _dup_heads(cv.reshape(b, -1, kv)).astype(BF))
                ob = _attn_call(per_seq(q), k_blocks, v_blocks)
                mix = ([oa], flat(ob), p["wo_a"], p["wo_b"])
            else:
                p = odd_weights(e)
                q, k, dd, vt, ot, gt = _odd_in_call(x, mod, gain, p["w"], p["wt"], p["wgt"], p["bg"])
                per_seq_t = lambda a: a.reshape((b, t // CHUNK) + a.shape[2:])
                c0, n0, m0 = ctx_odd[e]
                hf, hb, c1, n1, m1 = _mlstm_call(per_seq(q), per_seq(k), per_seq_t(vt), per_seq_t(ot),
                                                 per_seq_t(gt), c0, n0, m0)
                new_odd.append((c1, n1, m1))
                od = _fourier(per_seq(dd))
                mix = ([flat(hf), flat(hb)], flat(od), p["wo_a"], p["wo_b"])
            x = _ffn_call(x, mod, gain, ffn_in, ffn_out, l, 1, sub=2, mix=mix)
        return x, new_even, new_odd

    n_even = (depth + 1) // 2
    n_odd = depth // 2

    zero_state = (jnp.zeros((batch, N_STREAMS, C_DV, C_DK), F32), jnp.zeros((batch, N_STREAMS, C_DK), F32),
                  jnp.zeros((batch, N_STREAMS, CHUNK), F32))
    y, new_even, new_odd = run_pass(x_prompt.reshape(1, batch * seq, d_model), list(mod_ctx), batch, seq,
                                    None, None, [zero_state] * n_odd, True)
    y_prompt = y.reshape(batch, seq, d_model)
    new_cache_k = jnp.stack([k.reshape(batch, seq, B_KV, HEAD_DIM) for k, _ in new_even], axis=1)
    new_cache_v = jnp.stack([v.reshape(batch, seq, B_KV, HEAD_DIM) for _, v in new_even], axis=1)
    new_state_c = jnp.stack([c1.reshape(batch, N_DIR, C_HEADS, C_DV, C_DK) for c1, _, _ in new_odd], axis=1)
    new_state_n = jnp.stack([n1.reshape(batch, N_DIR, C_HEADS, C_DK) for _, n1, _ in new_odd], axis=1)
    new_state_m = jnp.stack([m1[:, :, 0].reshape(batch, N_DIR, C_HEADS) for _, _, m1 in new_odd], axis=1)

    ctx_even = [(cache_k[:, e], cache_v[:, e]) for e in range(n_even)]
    ctx_odd = []
    for e in range(n_odd):
        m0 = jnp.broadcast_to(state_m[:, e].reshape(dec_batch, N_STREAMS, 1), (dec_batch, N_STREAMS, CHUNK))
        ctx_odd.append((state_C[:, e].reshape(dec_batch, N_STREAMS, C_DV, C_DK),
                        state_n[:, e].reshape(dec_batch, N_STREAMS, C_DK), m0))
    y_sample, _, _ = run_pass(x_sample, list(mod_lat), dec_batch, dec_seq, _rope_tables(dec_seq),
                              ctx_even, ctx_odd, False)

    return (y_prompt, y_sample, new_cache_k, new_cache_v, new_state_c, new_state_n, new_state_m)
```

```python
import functools
import math

import jax
import jax.numpy as jnp
from jax import lax
from jax.experimental import pallas as pl
from jax.experimental.pallas import tpu as pltpu

BF = jnp.bfloat16
F32 = jnp.float32

EPS = 1e-6
ROPE_THETA = 10000.0
GRID_W = 64
CHUNK = 128
N_SUB = 3
A_GROUPS = 4
A_DIM = 128
A_WIDTH = A_GROUPS * A_DIM
B_HEADS = 8
B_KV = 2
HEAD_DIM = 64
B_WIDTH = B_HEADS * HEAD_DIM
C_HEADS = 4
C_DK = 128
C_DV = 128
C_WIDTH = C_HEADS * C_DV
N_DIR = 2
N_STREAMS = N_DIR * C_HEADS
D_GROUPS = 4
D_DIM = 128
D_WIDTH = D_GROUPS * D_DIM
LANES = 128

V7X_VMEM_BYTES = 64 * 1024 * 1024
VMEM_LIMIT = V7X_VMEM_BYTES - 4 * 1024 * 1024
ROW_TILE = 512
FFN_ROW_TILE = 1024


def _params(*sem):
    return pltpu.CompilerParams(dimension_semantics=sem, vmem_limit_bytes=VMEM_LIMIT)


def _dot(a, b):
    return jnp.dot(a, b, preferred_element_type=F32)


def _dot_nt(a, b):
    return lax.dot_general(a, b, (((1,), (1,)), ((), ())), preferred_element_type=F32)


def _dot_tn(a, b):
    return lax.dot_general(a, b, (((0,), (0,)), ((), ())), preferred_element_type=F32)


def _sigmoid(x):
    return 1.0 / (1.0 + jnp.exp(-x))


def _gelu_tanh(x):
    return 0.5 * x * (1.0 + jnp.tanh(math.sqrt(2.0 / math.pi) * (x + 0.044715 * (x * x * x))))


def _norm_mod(x, gain, shift, scale):
    ms = jnp.mean(x * x, axis=-1, keepdims=True)
    y = x * lax.rsqrt(ms + EPS) * gain
    return y * (1.0 + scale) + shift


def _const_spec(shape):
    zeros = (0,) * len(shape)
    return pl.BlockSpec(shape, lambda *_: zeros, pipeline_mode=pl.Buffered(1))


def _row_tile(t, target):
    tm = min(t, target)
    assert t % tm == 0
    return tm


def _mod_kernel(c_ref, w_ref, b_ref, o_ref):
    c = c_ref[...]
    sc = (c * _sigmoid(c)).astype(BF)
    o_ref[0] = _dot(sc, w_ref[0].astype(BF)) + b_ref[0]


def _mod_call(c_rows, w_mod, b_mod):
    depth, d, n = w_mod.shape
    rows = c_rows.shape[0]
    tn = 1024
    assert n % tn == 0
    return pl.pallas_call(
        _mod_kernel,
        grid=(depth, n // tn),
        in_specs=[
            pl.BlockSpec((rows, d), lambda l, j: (0, 0)),
            pl.BlockSpec((1, d, tn), lambda l, j: (l, 0, j)),
            pl.BlockSpec((1, 1, tn), lambda l, j: (l, 0, j)),
        ],
        out_specs=pl.BlockSpec((1, rows, tn), lambda l, j: (l, 0, j)),
        out_shape=jax.ShapeDtypeStruct((depth, rows, n), F32),
        compiler_params=_params("parallel", "parallel"),
        name="mod",
    )(c_rows, w_mod, b_mod.reshape(depth, 1, n))


FFN_CHUNK = 256


def _ffn_kernel(*refs, n_mix_a, sub):
    if n_mix_a:
        x_ref = refs[0]
        a_refs = refs[1:1 + n_mix_a]
        b_ref, mod_ref, gain_ref, wa_ref, wb_ref, win_ref, wout_ref, o_ref, act_ref = refs[1 + n_mix_a:]
    else:
        x_ref, mod_ref, gain_ref, win_ref, wout_ref, o_ref, act_ref = refs
    x = x_ref[0]
    if n_mix_a:
        if n_mix_a == 1:
            a = a_refs[0][0]
        else:
            a = (a_refs[0][0].astype(F32) + a_refs[1][0].astype(F32)).astype(BF)
        mix = _dot(a, wa_ref[...]) + _dot(b_ref[0], wb_ref[...])
        x = x + mod_ref[0, 5:6, :] * mix
    h = _norm_mod(x, gain_ref[sub:sub + 1, :], mod_ref[0, 3 * sub:3 * sub + 1, :],
                  mod_ref[0, 3 * sub + 1:3 * sub + 2, :]).astype(BF)
    d_ff = act_ref.shape[1]
    for c in range(d_ff // FFN_CHUNK):
        lo = c * FFN_CHUNK
        g = _dot(h, win_ref[:, lo:lo + FFN_CHUNK])
        u = _dot(h, win_ref[:, d_ff + lo:d_ff + lo + FFN_CHUNK])
        act_ref[:, lo:lo + FFN_CHUNK] = (g * _sigmoid(g) * u).astype(BF)
    y = _dot(act_ref[...], wout_ref[...])
    o_ref[0] = x + (0.5 * mod_ref[0, 3 * sub + 2:3 * sub + 3, :]) * y


def _ffn_call(x, mod, gain, w_in, w_out, layer, slot, sub, mix=None):
    bm, t, d = x.shape
    d_ff = w_out.shape[2]
    weight = lambda w: pl.BlockSpec((None, None) + w.shape[2:], lambda b, i: (layer, slot, 0, 0),
                                    pipeline_mode=pl.Buffered(1))
    tm = _row_tile(t, FFN_ROW_TILE)
    row = lambda w: pl.BlockSpec((1, tm, w), lambda b, i: (b, i, 0))
    in_specs = [row(d)]
    args = [x]
    n_mix_a = 0
    if mix is not None:
        a_list, b_arr, w_a, w_b = mix
        n_mix_a = len(a_list)
        for a in a_list:
            in_specs.append(row(a.shape[-1]))
            args.append(a)
        in_specs.append(row(b_arr.shape[-1]))
        args.append(b_arr)
    in_specs += [pl.BlockSpec((1, 3 * N_SUB, d), lambda b, i: (b, 0, 0)), _const_spec(gain.shape)]
    args += [mod, gain]
    if mix is not None:
        in_specs += [_const_spec(w_a.shape), _const_spec(w_b.shape)]
        args += [w_a, w_b]
    in_specs += [weight(w_in), weight(w_out)]
    args += [w_in, w_out]
    return pl.pallas_call(
        functools.partial(_ffn_kernel, n_mix_a=n_mix_a, sub=sub),
        grid=(bm, t // tm),
        in_specs=in_specs,
        out_specs=row(d),
        out_shape=jax.ShapeDtypeStruct((bm, t, d), F32),
        scratch_shapes=[pltpu.VMEM((tm, d_ff), BF)],
        compiler_params=_params("parallel", "parallel"),
        name="ffn_mix" if mix is not None else "ffn",
    )(*args)


Q_SCALE = HEAD_DIM ** -0.5 * math.log2(math.e)


def _rope(x, cos, sin_signed):
    w = x.shape[1]
    n = w // LANES
    cos = jnp.concatenate([cos] * n, axis=1)
    sin_signed = jnp.concatenate([sin_signed] * n, axis=1)
    lane = lax.broadcasted_iota(jnp.int32, (1, w), 1)
    first_half = (lane & (HEAD_DIM - 1)) < (HEAD_DIM // 2)
    fwd = pltpu.roll(x, w - HEAD_DIM // 2, axis=1)
    bwd = pltpu.roll(x, HEAD_DIM // 2, axis=1)
    return x * cos + jnp.where(first_half, fwd, bwd) * sin_signed


def _even_in_kernel(*refs, has_rope, is_ctx):
    it = iter(refs)
    x_ref, mod_ref, gain_ref, w_ref, sgu_ref, ws_ref, bs_ref, qg_ref, kg_ref, bd_ref = (next(it) for _ in range(10))
    if has_rope:
        cos_ref, sin_ref = next(it), next(it)
    oa_ref, q_ref, kd_ref, vd_ref = (next(it) for _ in range(4))
    if is_ctx:
        k32_ref, v32_ref = next(it), next(it)

    tm = x_ref.shape[1]
    nck = tm // CHUNK
    x = x_ref[0]
    h = _norm_mod(x, gain_ref[1:2, :], mod_ref[0, 3:4, :], mod_ref[0, 4:5, :]).astype(BF)

    off_q = 2 * A_WIDTH
    kvw = 2 * B_KV * HEAD_DIM
    off_k = off_q + B_WIDTH
    off_v = off_k + kvw
    au = _dot(h, w_ref[:, 0:A_WIDTH])
    av = _dot(h, w_ref[:, A_WIDTH:2 * A_WIDTH])
    q = _dot(h, w_ref[:, off_q:off_q + B_WIDTH])
    k = _dot(h, w_ref[:, off_k:off_k + kvw])
    v = _dot(h, w_ref[:, off_v:off_v + kvw])
    q_ms = _dot((q * q).astype(BF), bd_ref[...])
    k_ms = _dot((k * k).astype(BF), bd_ref[0:kvw, 0:kvw])

    av = _gelu_tanh(av)
    gates = []
    for g in range(A_GROUPS):
        cols = slice(g * A_DIM, (g + 1) * A_DIM)
        blk = av[:, cols]
        ms = jnp.mean(blk * blk, axis=-1, keepdims=True)
        vg = (blk * lax.rsqrt(ms + EPS) * sgu_ref[:, cols]).astype(BF)
        rhs = jnp.concatenate([vg[c * CHUNK:(c + 1) * CHUNK, :] for c in range(nck)], axis=1)
        gates.append(_dot(ws_ref[g], rhs))

    lane = lax.broadcasted_iota(jnp.int32, (1, LANES), 1)
    lo_half = lane < HEAD_DIM
    if has_rope:
        cos = cos_ref[...]
        sin = sin_ref[...]
    q = q * lax.rsqrt(q_ms + EPS) * qg_ref[...]
    if has_rope:
        q = _rope(q, cos, sin)
    q_ref[0] = (q * Q_SCALE).astype(BF)
    k = k * lax.rsqrt(k_ms + EPS) * kg_ref[...]
    if is_ctx:
        k32_ref[0] = jnp.where(lo_half, k[:, 0:LANES], k[:, LANES:2 * LANES])
        v32_ref[0] = jnp.where(lo_half, v[:, 0:LANES], v[:, LANES:2 * LANES])
    if has_rope:
        k = _rope(k, cos, sin)
    kd_ref[0] = k.astype(BF)
    spare = (lax.broadcasted_iota(jnp.int32, (1, kvw), 1) & (LANES - 1)) >= HEAD_DIM
    vd_ref[0] = jnp.where(spare, 1.0, v).astype(BF)
    au = _gelu_tanh(au)
    for g in range(A_GROUPS):
        cols = slice(g * A_DIM, (g + 1) * A_DIM)
        for c in range(nck):
            rows = slice(c * CHUNK, (c + 1) * CHUNK)
            gate = gates[g][:, c * A_DIM:(c + 1) * A_DIM] + bs_ref[g]
            oa_ref[0, rows, cols] = (au[rows, cols] * gate).astype(BF)


def _even_in_call(x, mod, gain, w, sgu, ws, bs, qg, kg, bd, rope, is_ctx):
    bm, t, d = x.shape
    tm = _row_tile(t, ROW_TILE)
    kvw = 2 * B_KV * HEAD_DIM
    row = lambda w_: pl.BlockSpec((1, tm, w_), lambda b, i: (b, i, 0))
    in_specs = [row(d), pl.BlockSpec((1, 3 * N_SUB, d), lambda b, i: (b, 0, 0))]
    in_specs += [_const_spec(a.shape) for a in (gain, w, sgu, ws, bs, qg, kg, bd)]
    args = [x, mod, gain, w, sgu, ws, bs, qg, kg, bd]
    if rope is not None:
        in_specs += [pl.BlockSpec((tm, LANES), lambda b, i: (i, 0))] * 2
        args += list(rope)
    out_specs = [row(A_WIDTH), row(B_WIDTH), row(kvw), row(kvw)]
    out_shape = [jax.ShapeDtypeStruct((bm, t, A_WIDTH), BF), jax.ShapeDtypeStruct((bm, t, B_WIDTH), BF),
                 jax.ShapeDtypeStruct((bm, t, kvw), BF), jax.ShapeDtypeStruct((bm, t, kvw), BF)]
    if is_ctx:
        out_specs += [row(B_KV * HEAD_DIM)] * 2
        out_shape += [jax.ShapeDtypeStruct((bm, t, B_KV * HEAD_DIM), F32)] * 2
    return pl.pallas_call(
        functools.partial(_even_in_kernel, has_rope=rope is not None, is_ctx=is_ctx),
        grid=(bm, t // tm),
        in_specs=in_specs,
        out_specs=out_specs,
        out_shape=out_shape,
        compiler_params=_params("parallel", "parallel"),
        name="even_in",
    )(*args)


ATTN_ROWS = 128
ATTN_SHORT_KEYS = 1024


def _attn_kernel(*refs, n_kv):
    q_ref = refs[0]
    k_refs = refs[1:1 + n_kv]
    v_refs = refs[1 + n_kv:1 + 2 * n_kv]
    o_ref = refs[1 + 2 * n_kv]
    if n_kv > 1:
        k_all, v_all = refs[2 + 2 * n_kv:]

        @pl.when(pl.program_id(1) == 0)
        def _():
            off = 0
            for k_ref, v_ref in zip(k_refs, v_refs):
                n = k_ref.shape[1]
                k_all[off:off + n, :] = k_ref[0]
                v_all[off:off + n, :] = v_ref[0]
                off += n

        n_keys = k_all.shape[0]
        keys = lambda cols: k_all[:, cols]
        vals = lambda cols: v_all[:, cols]
    else:
        n_keys = k_refs[0].shape[1]
        keys = lambda cols: k_refs[0][0, :, cols]
        vals = lambda cols: v_refs[0][0, :, cols]
    tq = q_ref.shape[1]
    lane = lax.broadcasted_iota(jnp.int32, (1, LANES), 1)
    lo_half = lane < HEAD_DIM
    heads_per_kv = B_HEADS // B_KV
    units = [(r, h) for r in range(tq // ATTN_ROWS) for h in range(B_HEADS)]

    def kv_cols(h):
        kv = h // heads_per_kv
        return slice(kv * LANES, (kv + 1) * LANES)

    def scores(unit):
        r, h = unit
        p = h // 2
        qp = q_ref[0, r * ATTN_ROWS:(r + 1) * ATTN_ROWS, p * LANES:(p + 1) * LANES]
        keep = lo_half if h % 2 == 0 else jnp.logical_not(lo_half)
        return _dot_nt(jnp.where(keep, qp, jnp.zeros_like(qp)), keys(kv_cols(h)))

    def weights(s):
        return jnp.exp2(s - jnp.max(s, axis=-1, keepdims=True)).astype(BF)

    def attend(e, unit):
        o2 = _dot(e, vals(kv_cols(unit[1])))
        return o2 / pltpu.roll(o2, HEAD_DIM, axis=1)

    outs = {}
    if n_keys <= ATTN_SHORT_KEYS:
        es = [weights(s) for s in [scores(u) for u in units]]
        for e, unit in zip(es, units):
            outs[unit] = attend(e, unit)
    else:
        s_next = scores(units[0])
        pending = None
        for i, unit in enumerate(units):
            s = s_next
            if i + 1 < len(units):
                s_next = scores(units[i + 1])
            e = weights(s)
            if pending is not None:
                outs[pending[1]] = attend(*pending)
            pending = (e, unit)
        outs[pending[1]] = attend(*pending)
    for r in range(tq // ATTN_ROWS):
        for p in range(B_WIDTH // LANES):
            o_ref[0, r * ATTN_ROWS:(r + 1) * ATTN_ROWS, p * LANES:(p + 1) * LANES] = jnp.where(
                lo_half, outs[(r, 2 * p)], pltpu.roll(outs[(r, 2 * p + 1)], HEAD_DIM, axis=1)).astype(BF)


def _attn_call(q, k_blocks, v_blocks):
    b, t, w = q.shape
    tq = _row_tile(t, ROW_TILE)
    whole = lambda a: pl.BlockSpec((1,) + a.shape[1:], lambda i, j: (i, 0, 0))
    n_kv = len(k_blocks)
    scratch = []
    if n_kv > 1:
        s_all = sum(a.shape[1] for a in k_blocks)
        scratch = [pltpu.VMEM((s_all, k_blocks[0].shape[2]), BF), pltpu.VMEM((s_all, v_blocks[0].shape[2]), BF)]
    return pl.pallas_call(
        functools.partial(_attn_kernel, n_kv=n_kv),
        grid=(b, t // tq),
        in_specs=[pl.BlockSpec((1, tq, w), lambda i, j: (i, j, 0))]
        + [whole(a) for a in k_blocks] + [whole(a) for a in v_blocks],
        out_specs=pl.BlockSpec((1, tq, w), lambda i, j: (i, j, 0)),
        out_shape=jax.ShapeDtypeStruct((b, t, w), BF),
        scratch_shapes=scratch,
        compiler_params=_params("parallel", "arbitrary"),
        name="attn",
    )(q, *k_blocks, *v_blocks)


N_GATE_ROWS = 2 * N_STREAMS


def _log_sigmoid(x):
    return jnp.minimum(x, 0.0) - jnp.log(1.0 + jnp.exp(-jnp.abs(x)))


def _odd_in_kernel(x_ref, mod_ref, gain_ref, w_ref, wt_ref, wgt_ref, bg_ref,
                   q_ref, k_ref, d_ref, vt_ref, ot_ref, gt_ref):
    tm = x_ref.shape[1]
    x = x_ref[0]
    h = _norm_mod(x, gain_ref[1:2, :], mod_ref[0, 3:4, :], mod_ref[0, 4:5, :]).astype(BF)
    w = C_WIDTH
    q_ref[0] = _dot(h, w_ref[:, 0:w]).astype(BF)
    k_ref[0] = (_dot(h, w_ref[:, w:2 * w]) * (C_DK ** -0.5)).astype(BF)
    d_ref[0] = _dot(h, w_ref[:, 2 * w:2 * w + D_WIDTH]).astype(BF)
    vt = _dot_nt(wt_ref[0:w, :], h)
    ot = _sigmoid(_dot_nt(wt_ref[w:2 * w, :], h))
    gt = _dot_nt(wgt_ref[...], h)
    row = lax.broadcasted_iota(jnp.int32, (N_GATE_ROWS, CHUNK), 0)
    for c in range(tm // CHUNK):
        lanes = slice(c * CHUNK, (c + 1) * CHUNK)
        vt_ref[0, c] = vt[:, lanes].astype(BF)
        ot_ref[0, c] = ot[:, lanes].astype(BF)
        g = gt[:, lanes] + bg_ref[...]
        gt_ref[0, c] = jnp.where(row >= N_STREAMS, _log_sigmoid(g), g)


def _odd_in_call(x, mod, gain, w, wt, wgt, bg):
    bm, t, d = x.shape
    tm = _row_tile(t, ROW_TILE)
    nck = tm // CHUNK
    row = lambda w_: pl.BlockSpec((1, tm, w_), lambda b, i: (b, i, 0))
    chunked = lambda r: pl.BlockSpec((1, nck, r, CHUNK), lambda b, i: (b, i, 0, 0))
    return pl.pallas_call(
        _odd_in_kernel,
        grid=(bm, t // tm),
        in_specs=[row(d), pl.BlockSpec((1, 3 * N_SUB, d), lambda b, i: (b, 0, 0))]
        + [_const_spec(a.shape) for a in (gain, w, wt, wgt, bg)],
        out_specs=[row(C_WIDTH), row(C_WIDTH), row(D_WIDTH), chunked(C_WIDTH), chunked(C_WIDTH),
                   chunked(N_GATE_ROWS)],
        out_shape=[jax.ShapeDtypeStruct((bm, t, C_WIDTH), BF), jax.ShapeDtypeStruct((bm, t, C_WIDTH), BF),
                   jax.ShapeDtypeStruct((bm, t, D_WIDTH), BF),
                   jax.ShapeDtypeStruct((bm, t // CHUNK, C_WIDTH, CHUNK), BF),
                   jax.ShapeDtypeStruct((bm, t // CHUNK, C_WIDTH, CHUNK), BF),
                   jax.ShapeDtypeStruct((bm, t // CHUNK, N_GATE_ROWS, CHUNK), F32)],
        compiler_params=_params("parallel", "parallel"),
        name="odd_in",
    )(x, mod, gain, w, wt, wgt, bg)


N_REP = 8
CN_ROWS = C_DV + N_REP


def _split3(x):
    hi = x.astype(BF)
    r1 = x - hi.astype(F32)
    mid = r1.astype(BF)
    lo = (r1 - mid.astype(F32)).astype(BF)
    return hi, mid, lo


def _mlstm_kernel(qf_ref, kf_ref, vtf_ref, otf_ref, gtf_ref, qb_ref, kb_ref, vtb_ref, otb_ref, gtb_ref,
                  c0_ref, n0_ref, m0_ref,
                  hf_ref, hb_ref, c1_ref, n1_ref, m1_ref,
                  cn_s, m_s):
    i = pl.program_id(1)
    nck = qf_ref.shape[1] // CHUNK

    @pl.when(i == 0)
    def _():
        for sl in range(N_STREAMS):
            cn_s[sl, 0:C_DV, :] = c0_ref[0, sl]
            cn_s[sl, C_DV:CN_ROWS, :] = jnp.broadcast_to(n0_ref[0, sl:sl + 1, :], (N_REP, C_DK))
        m_s[...] = m0_ref[0]

    r_i = lax.broadcasted_iota(jnp.int32, (CHUNK, CHUNK), 0)
    c_i = lax.broadcasted_iota(jnp.int32, (CHUNK, CHUNK), 1)
    upper = c_i >= r_i
    lower = c_i <= r_i
    tri_up = jnp.where(upper, 1.0, 0.0).astype(BF)
    tri_lo = jnp.where(lower, 1.0, 0.0).astype(BF)
    eye = jnp.where(c_i == r_i, 1.0, 0.0).astype(BF)
    ones = jnp.ones((CHUNK, CHUNK), BF)
    n_rows = nck * N_STREAMS
    assert n_rows <= CHUNK
    is_fwd = (lax.broadcasted_iota(jnp.int32, (n_rows, CHUNK), 0) & (N_STREAMS - 1)) < C_HEADS
    is_fwd8 = is_fwd[0:N_STREAMS]
    lane = lax.broadcasted_iota(jnp.int32, (n_rows, CHUNK), 1)

    g_i, g_f = [], []
    for c in range(nck):
        gf = gtf_ref[0, c]
        gb = gtb_ref[0, nck - 1 - c]
        g_i.append(jnp.where(is_fwd8, gf[0:N_STREAMS], gb[0:N_STREAMS]))
        g_f.append(jnp.where(is_fwd8, gf[N_STREAMS:N_GATE_ROWS], gb[N_STREAMS:N_GATE_ROWS]))
    g_i = jnp.concatenate(g_i, axis=0)
    g_f = jnp.concatenate(g_f, axis=0)
    parts = _split3(g_f)
    cum = jnp.where(is_fwd, sum(_dot(p, tri_up) for p in parts),
                    sum(_dot(p, tri_lo) for p in parts))
    total = sum(_dot(p, ones) for p in parts)
    r = g_i - cum
    run = r
    sh = 1
    while sh < CHUNK:
        prev = jnp.where(lane >= sh, pltpu.roll(run, sh, axis=1), -jnp.inf)
        nxt = jnp.where(lane < CHUNK - sh, pltpu.roll(run, CHUNK - sh, axis=1), -jnp.inf)
        run = jnp.maximum(run, jnp.where(is_fwd, prev, nxt))
        sh *= 2
    log_s = total - cum + g_i
    ls_max = jnp.max(log_s, axis=-1, keepdims=True)
    m = m_s[...]
    m_old, m_new = [], []
    for c in range(nck):
        rows = slice(c * N_STREAMS, (c + 1) * N_STREAMS)
        m_old.append(m)
        m = jnp.maximum(total[rows] + m, ls_max[rows])
        m_new.append(m)
    m_s[...] = m
    m_old = jnp.concatenate(m_old, axis=0)
    m_new = jnp.concatenate(m_new, axis=0)
    log_inter = cum + m_old
    m_t = jnp.maximum(log_inter, cum + run)
    bm = cum - m_t
    w_prev = jnp.exp(log_inter - m_t)
    e_inv = jnp.exp(-m_t)
    w_src = jnp.exp(log_s - m_new)
    w_keep = jnp.exp(total + m_old - m_new)
    r_cols = jnp.concatenate([r, jnp.zeros((CHUNK - n_rows, CHUNK), F32)], axis=0).T

    row = lambda a, j: a[j:j + 1, :]
    units = []
    for c in range(nck):
        for sl in range(N_STREAMS):
            fwd = sl < C_HEADS
            q_ref, k_ref, vt_ref, ot_ref, h_ref = ((qf_ref, kf_ref, vtf_ref, otf_ref, hf_ref) if fwd
                                                   else (qb_ref, kb_ref, vtb_ref, otb_ref, hb_ref))
            ck = c if fwd else nck - 1 - c
            rows = slice(ck * CHUNK, (ck + 1) * CHUNK)
            cols = slice((sl % C_HEADS) * C_DK, (sl % C_HEADS + 1) * C_DK)
            units.append((c * N_STREAMS + sl, sl, q_ref[0, rows, cols], k_ref[0, rows, cols],
                          vt_ref[0, ck, cols, :], ot_ref[0, ck, cols, :], h_ref, rows, cols))
    s_t = [_dot_nt(k, q) for _, _, q, k, _, _, _, _, _ in units]
    upd = []
    for j, _, _, k, vt, _, _, _, _ in units:
        ws = row(w_src, j)
        lhs = jnp.concatenate([(vt.astype(F32) * ws).astype(BF),
                               jnp.broadcast_to(ws, (N_REP, CHUNK)).astype(BF)], axis=0)
        upd.append(_dot(lhs, k))
    a_t = []
    for (j, sl, *_), s in zip(units, s_t):
        mask = upper if sl < C_HEADS else lower
        a_t.append(jnp.exp(jnp.where(mask, r_cols[:, j:j + 1] + row(bm, j), -jnp.inf)) * s)
    num = [_dot(u[4], a.astype(BF)) for u, a in zip(units, a_t)]
    cn = [cn_s[sl] for sl in range(N_STREAMS)]
    step_units = lambda c: range(c * N_STREAMS, (c + 1) * N_STREAMS)

    def read_state(c):
        out = []
        for idx in step_units(c):
            j, sl, q = units[idx][0:3]
            out.append(_dot_nt(cn[sl].astype(BF), q))
            cn[sl] = row(w_keep, j) * cn[sl] + upd[idx]
        return out

    inter = read_state(0)
    for c in range(nck):
        h_t = []
        for idx, it in zip(step_units(c), inter):
            j, ot = units[idx][0], units[idx][5]
            wp = row(w_prev, j)
            den = jnp.sum(a_t[idx], axis=0, keepdims=True) + wp * it[C_DV:C_DV + 1]
            denom = jnp.maximum(jnp.abs(den), row(e_inv, j))
            h_t.append(((num[idx] + wp * it[0:C_DV]) * (1.0 / denom) * ot.astype(F32)).astype(BF))
        if c + 1 < nck:
            inter = read_state(c + 1)
        h_rows = [_dot_nt(eye, h).astype(BF) for h in h_t]
        for idx, h in zip(step_units(c), h_rows):
            h_ref, rows, cols = units[idx][6:9]
            h_ref[0, rows, cols] = h
    for sl in range(N_STREAMS):
        cn_s[sl] = cn[sl]

    @pl.when(i == pl.num_programs(1) - 1)
    def _():
        for sl in range(N_STREAMS):
            c1_ref[0, sl] = cn_s[sl, 0:C_DV, :]
            n1_ref[0, sl:sl + 1, :] = cn_s[sl, C_DV:C_DV + 1, :]
        m1_ref[0] = m_s[...]


def _mlstm_call(q, k, vt, ot, gt, c0, n0, m0):
    b, t, w = q.shape
    tb = _row_tile(t, ROW_TILE)
    nt = t // tb
    nck = tb // CHUNK
    fwd = pl.BlockSpec((1, tb, w), lambda i, j: (i, j, 0))
    bwd = pl.BlockSpec((1, tb, w), lambda i, j: (i, nt - 1 - j, 0))
    fwd_t = lambda r: pl.BlockSpec((1, nck, r, CHUNK), lambda i, j: (i, j, 0, 0))
    bwd_t = lambda r: pl.BlockSpec((1, nck, r, CHUNK), lambda i, j: (i, nt - 1 - j, 0, 0))
    st_c = pl.BlockSpec((1, N_STREAMS, C_DV, C_DK), lambda i, j: (i, 0, 0, 0))
    st_n = pl.BlockSpec((1, N_STREAMS, C_DK), lambda i, j: (i, 0, 0))
    st_m = pl.BlockSpec((1, N_STREAMS, CHUNK), lambda i, j: (i, 0, 0))
    return pl.pallas_call(
        _mlstm_kernel,
        grid=(b, nt),
        in_specs=[fwd, fwd, fwd_t(w), fwd_t(w), fwd_t(N_GATE_ROWS), bwd, bwd, bwd_t(w), bwd_t(w),
                  bwd_t(N_GATE_ROWS), st_c, st_n, st_m],
        out_specs=[fwd, bwd, st_c, st_n, st_m],
        out_shape=[jax.ShapeDtypeStruct((b, t, w), BF), jax.ShapeDtypeStruct((b, t, w), BF),
                   jax.ShapeDtypeStruct(c0.shape, F32), jax.ShapeDtypeStruct(n0.shape, F32),
                   jax.ShapeDtypeStruct(m0.shape, F32)],
        scratch_shapes=[pltpu.VMEM((N_STREAMS, CN_ROWS, C_DK), F32), pltpu.VMEM((N_STREAMS, CHUNK), F32)],
        compiler_params=_params("parallel", "arbitrary"),
        name="mlstm",
    )(q, k, vt, ot, gt, q, k, vt, ot, gt, c0, n0, m0)


def _fourier_kernel(d_ref, dft_t_ref, dft_c_ref, o_ref, y_s, *, scale):
    t = d_ref.shape[1]

    @pl.when(pl.program_id(1) == 0)
    def _():
        step = min(t, 512)
        for r in range(t // step):
            rows = slice(r * step, (r + 1) * step)
            for g in range(D_GROUPS):
                cols = slice(g * D_DIM, (g + 1) * D_DIM)
                y = _dot(d_ref[0, rows, cols], dft_c_ref[...])
                y_s[rows, cols] = y[:, 0:D_DIM].astype(BF)
                y_s[t + r * step:t + (r + 1) * step, cols] = y[:, D_DIM:2 * D_DIM].astype(BF)

    o_ref[0] = (_dot(dft_t_ref[...], y_s[...]) * scale).astype(BF)


def _fourier_call(d, dft_t, dft_c):
    b, t, w = d.shape
    tr = _row_tile(t, ROW_TILE)
    return pl.pallas_call(
        functools.partial(_fourier_kernel, scale=float((t * D_DIM) ** -0.5)),
        grid=(b, t // tr),
        in_specs=[
            pl.BlockSpec((1, t, w), lambda i, j: (i, 0, 0)),
            pl.BlockSpec((tr, 2 * t), lambda i, j: (j, 0)),
            _const_spec(dft_c.shape),
        ],
        out_specs=pl.BlockSpec((1, tr, w), lambda i, j: (i, j, 0)),
        out_shape=jax.ShapeDtypeStruct((b, t, w), BF),
        scratch_shapes=[pltpu.VMEM((2 * t, w), BF)],
        compiler_params=_params("parallel", "arbitrary"),
        name="fourier",
    )(d, dft_t, dft_c)


DFT_COLS = 16
DFT_RESIDUES = 8


def _dft_rows_kernel(x_ref, f_ref, zr_ref, zi_ref):
    n1, nc, w = x_ref.shape[1:]
    z = _dot(f_ref[...], x_ref[0].reshape(n1 * nc, w))
    zr_ref[0] = z[0:n1 * nc].astype(BF).reshape(n1, nc, w)
    zi_ref[0] = z[n1 * nc:2 * n1 * nc].astype(BF).reshape(n1, nc, w)


def _dft_rows_call(x4, f1):
    b, n1, n2, w = x4.shape
    spec = pl.BlockSpec((1, n1, DFT_COLS, w), lambda i, j: (i, 0, j, 0))
    return pl.pallas_call(
        _dft_rows_kernel,
        grid=(b, n2 // DFT_COLS),
        in_specs=[spec, _const_spec(f1.shape)],
        out_specs=[spec, spec],
        out_shape=[jax.ShapeDtypeStruct(x4.shape, BF)] * 2,
        compiler_params=_params("parallel", "parallel"),
        name="dft_rows",
    )(x4, f1)


def _dft_cols_kernel(zr_ref, zi_ref, g_ref, c_ref, o_ref, *, scale):
    n2 = GRID_W
    nb = zr_ref.shape[1]
    w = zr_ref.shape[3]
    xr, xi = [], []
    for r in range(nb):
        z = jnp.concatenate([zr_ref[0, r], zi_ref[0, r]], axis=0)
        x = _dot(g_ref[r], z)
        xr.append(x[0:n2])
        xi.append(x[n2:2 * n2])
    xr = jnp.concatenate(xr, axis=0).astype(BF)
    xi = jnp.concatenate(xi, axis=0).astype(BF)
    for g in range(D_GROUPS):
        cols = slice(g * D_DIM, (g + 1) * D_DIM)
        y = _dot(jnp.concatenate([xr[:, cols], xi[:, cols]], axis=1), c_ref[...]) * scale
        for r in range(nb):
            o_ref[0, :, r * w + g * D_DIM:r * w + (g + 1) * D_DIM] = y[r * n2:(r + 1) * n2].astype(BF)


def _dft_cols_call(zr, zi, gmat, cmat, scale):
    b, n1, n2, w = zr.shape
    nb = DFT_RESIDUES
    zspec = pl.BlockSpec((1, nb, n2, w), lambda i, j: (i, j, 0, 0))
    return pl.pallas_call(
        functools.partial(_dft_cols_kernel, scale=scale),
        grid=(b, n1 // nb),
        in_specs=[zspec, zspec, pl.BlockSpec((nb, 2 * n2, 2 * n2), lambda i, j: (j, 0, 0)),
                  _const_spec(cmat.shape)],
        out_specs=pl.BlockSpec((1, n2, nb * w), lambda i, j: (i, 0, j)),
        out_shape=jax.ShapeDtypeStruct((b, n2, n1 * w), BF),
        compiler_params=_params("parallel", "parallel"),
        name="dft_cols",
    )(zr, zi, gmat, cmat)


def _cos_sin(num, den):
    ang = (num % den).astype(F32) * (2.0 * math.pi / den)
    return jnp.cos(ang), jnp.sin(ang)


def _dft_cos_sin(n):
    k = jnp.arange(n, dtype=jnp.int32)
    return _cos_sin(k[:, None] * k[None, :], n)


DENSE_DFT_MAX = 1024


def _fourier(d):
    b, t, w = d.shape
    cc, sc = _dft_cos_sin(D_DIM)
    scale = float((t * D_DIM) ** -0.5)
    if t <= DENSE_DFT_MAX:
        ct, st = _dft_cos_sin(t)
        return _fourier_call(d, jnp.concatenate([ct, -st], axis=1).astype(BF),
                             jnp.concatenate([cc, sc], axis=1).astype(BF))
    n2 = GRID_W
    n1 = t // n2
    assert n1 * n2 == t and n1 % DFT_RESIDUES == 0 and n2 % DFT_COLS == 0
    a1 = jnp.arange(n1, dtype=jnp.int32)
    a2 = jnp.arange(n2, dtype=jnp.int32)
    c1, s1 = _cos_sin(a1[:, None] * a1[None, :], n1)
    idx = jnp.arange(n1 * DFT_COLS, dtype=jnp.int32)
    rep = (idx[:, None] // DFT_COLS == a1[None, :]).astype(BF)
    same_col = (idx[:, None] % DFT_COLS) == (idx[None, :] % DFT_COLS)
    expand = lambda f: jnp.where(same_col, _dot(_dot(rep, f.astype(BF)).astype(BF), rep.T), 0.0)
    f1 = jnp.concatenate([expand(c1), expand(-s1)], axis=0).astype(BF)
    tp = a1[:, None, None] + n1 * a2[None, :, None]
    gc, gs = _cos_sin(tp * a2[None, None, :], t)
    gmat = jnp.concatenate([jnp.concatenate([gc, gs], axis=2),
                            jnp.concatenate([-gs, gc], axis=2)], axis=1).astype(BF)
    cmat = jnp.concatenate([cc, sc], axis=0).astype(BF)
    zr, zi = _dft_rows_call(d.reshape(b, n1, n2, w), f1)
    out = _dft_cols_call(zr, zi, gmat, cmat, scale)
    return out.reshape(b, t, w)


def _rope_tables(t):
    rows = t // GRID_W
    r = jnp.repeat(jnp.arange(rows), GRID_W).astype(F32)
    cidx = jnp.tile(jnp.arange(GRID_W), rows).astype(F32)
    n_freq = HEAD_DIM // 4
    inv = ROPE_THETA ** (-jnp.arange(n_freq, dtype=F32) / n_freq)
    ang = jnp.concatenate([r[:, None] * inv, cidx[:, None] * inv], axis=-1)
    ang = jnp.concatenate([ang] * (2 * LANES // HEAD_DIM), axis=-1)
    lane = jnp.arange(LANES)
    sign = jnp.where((lane % HEAD_DIM) < HEAD_DIM // 2, -1.0, 1.0).astype(F32)
    return jnp.cos(ang), jnp.sin(ang) * sign


def _dup_heads(a, ones=False):
    parts = []
    for h in range(B_KV):
        blk = a[..., h * HEAD_DIM:(h + 1) * HEAD_DIM]
        parts += [blk, jnp.ones_like(blk) if ones else blk]
    return jnp.concatenate(parts, axis=-1)


def kernel(x_prompt, x_sample, c, cache_k, cache_v, state_C, state_n, state_m, c_ctx, norm_gain, w_mod, b_mod,
           ffn_w_in, ffn_w_out, w_in_even, w_out_even, spatial_w, spatial_b, sgu_gain, q_gain, k_gain,
           w_in_odd, b_gate_odd, w_out_odd):
    depth = norm_gain.shape[0]
    batch, seq, d_model = x_prompt.shape
    dec_batch, dec_seq, _ = x_sample.shape

    rows = 1 + dec_batch
    rows_pad = -(-rows // 8) * 8
    c_rows = jnp.concatenate([c_ctx[None, :], c, jnp.zeros((rows_pad - rows, d_model), F32)], axis=0)
    mod_all = _mod_call(c_rows, w_mod, b_mod)
    mod_ctx = mod_all[:, 0:1].reshape(depth, 1, 3 * N_SUB, d_model)
    mod_lat = mod_all[:, 1:rows].reshape(depth, dec_batch, 3 * N_SUB, d_model)

    ffn_in = ffn_w_in.astype(BF)
    ffn_out = ffn_w_out.astype(BF)

    kv = B_KV * HEAD_DIM
    lane = jnp.arange(LANES)
    bd = ((jnp.arange(B_WIDTH)[:, None] // HEAD_DIM) == (jnp.arange(B_WIDTH)[None, :] // HEAD_DIM))
    bd = jnp.where(bd, 1.0 / HEAD_DIM, 0.0).astype(BF)

    def even_weights(e):
        w = w_in_even[e]
        o = 2 * A_WIDTH + B_WIDTH
        w_cat = jnp.concatenate([w[:, :o], _dup_heads(w[:, o:o + kv]), _dup_heads(w[:, o + kv:o + 2 * kv])], axis=1)
        bs = jnp.broadcast_to(spatial_b[e][:, :, None], (A_GROUPS, CHUNK, A_DIM)).astype(F32)
        return dict(
            w=w_cat.astype(BF), sgu=sgu_gain[e].reshape(1, A_WIDTH), ws=spatial_w[e].astype(BF), bs=bs,
            qg=jnp.tile(q_gain[e], B_HEADS)[None, :], kg=jnp.tile(k_gain[e], 2 * B_KV)[None, :],
            wo_a=w_out_even[e][:A_WIDTH].astype(BF), wo_b=w_out_even[e][A_WIDTH:].astype(BF))

    def odd_weights(e):
        w = w_in_odd[e]
        o = 2 * C_HEADS * C_DK + 2 * C_WIDTH
        n_g = 2 * N_STREAMS
        qk = 2 * C_HEADS * C_DK
        w_rows = jnp.concatenate([w[:, :qk], w[:, o + n_g:]], axis=1)
        w_t = w[:, qk:o].T
        perm = jnp.array([d * 2 * C_HEADS + g * C_HEADS + h
                          for g in range(2) for d in range(N_DIR) for h in range(C_HEADS)])
        wg_t = w[:, o:o + n_g][:, perm].T
        bg = jnp.broadcast_to(b_gate_odd[e].reshape(-1)[perm][:, None], (n_g, CHUNK)).astype(F32)
        return dict(w=w_rows.astype(BF), wt=w_t.astype(BF), wgt=wg_t.astype(BF), bg=bg,
                    wo_a=w_out_odd[e][:C_WIDTH].astype(BF), wo_b=w_out_odd[e][C_WIDTH:].astype(BF))

    def run_pass(x, mods, b, t, rope, ctx_even, ctx_odd, is_ctx):
        bm = x.shape[0]
        per_seq = lambda a: a.reshape(b, t, a.shape[-1])
        flat = lambda a: a.reshape(bm, -1, a.shape[-1])
        new_even, new_odd = [], []
        for l in range(depth):
            e = l // 2
            mod, gain = mods[l], norm_gain[l]
            x = _ffn_call(x, mod, gain, ffn_in, ffn_out, l, 0, sub=0)
            if l % 2 == 0:
                p = even_weights(e)
                outs = _even_in_call(x, mod, gain, p["w"], p["sgu"], p["ws"], p["bs"], p["qg"], p["kg"], bd,
                                     rope, is_ctx)
                oa, q, kd, vd = outs[:4]
                k_blocks, v_blocks = [per_seq(kd)], [per_seq(vd)]
                if is_ctx:
                    new_even.append((outs[4], outs[5]))
                else:
                    ck, cv = ctx_even[e]
                    k_blocks.insert(0, _dup_heads(ck.reshape(b, -1, kv)).astype(BF))
                    v_blocks.insert(0, _dup_heads(cv.reshape(b, -1, kv), ones=True).astype(BF))
                ob = _attn_call(per_seq(q), k_blocks, v_blocks)
                mix = ([oa], flat(ob), p["wo_a"], p["wo_b"])
            else:
                p = odd_weights(e)
                q, k, dd, vt, ot, gt = _odd_in_call(x, mod, gain, p["w"], p["wt"], p["wgt"], p["bg"])
                per_seq_t = lambda a: a.reshape((b, t // CHUNK) + a.shape[2:])
                c0, n0, m0 = ctx_odd[e]
                hf, hb, c1, n1, m1 = _mlstm_call(per_seq(q), per_seq(k), per_seq_t(vt), per_seq_t(ot),
                                                 per_seq_t(gt), c0, n0, m0)
                new_odd.append((c1, n1, m1))
                od = _fourier(per_seq(dd))
                mix = ([flat(hf), flat(hb)], flat(od), p["wo_a"], p["wo_b"])
            x = _ffn_call(x, mod, gain, ffn_in, ffn_out, l, 1, sub=2, mix=mix)
        return x, new_even, new_odd

    n_even = (depth + 1) // 2
    n_odd = depth // 2

    zero_state = (jnp.zeros((batch, N_STREAMS, C_DV, C_DK), F32), jnp.zeros((batch, N_STREAMS, C_DK), F32),
                  jnp.zeros((batch, N_STREAMS, CHUNK), F32))
    y, new_even, new_odd = run_pass(x_prompt.reshape(1, batch * seq, d_model), list(mod_ctx), batch, seq,
                                    None, None, [zero_state] * n_odd, True)
    y_prompt = y.reshape(batch, seq, d_model)
    new_cache_k = jnp.stack([k.reshape(batch, seq, B_KV, HEAD_DIM) for k, _ in new_even], axis=1)
    new_cache_v = jnp.stack([v.reshape(batch, seq, B_KV, HEAD_DIM) for _, v in new_even], axis=1)
    new_state_c = jnp.stack([c1.reshape(batch, N_DIR, C_HEADS, C_DV, C_DK) for c1, _, _ in new_odd], axis=1)
    new_state_n = jnp.stack([n1.reshape(batch, N_DIR, C_HEADS, C_DK) for _, n1, _ in new_odd], axis=1)
    new_state_m = jnp.stack([m1[:, :, 0].reshape(batch, N_DIR, C_HEADS) for _, _, m1 in new_odd], axis=1)

    ctx_even = [(cache_k[:, e], cache_v[:, e]) for e in range(n_even)]
    ctx_odd = []
    for e in range(n_odd):
        m0 = jnp.broadcast_to(state_m[:, e].reshape(dec_batch, N_STREAMS, 1), (dec_batch, N_STREAMS, CHUNK))
        ctx_odd.append((state_C[:, e].reshape(dec_batch, N_STREAMS, C_DV, C_DK),
                        state_n[:, e].reshape(dec_batch, N_STREAMS, C_DK), m0))
    y_sample, _, _ = run_pass(x_sample, list(mod_lat), dec_batch, dec_seq, _rope_tables(dec_seq),
                              ctx_even, ctx_odd, False)

    return (y_prompt, y_sample, new_cache_k, new_cache_v, new_state_c, new_state_n, new_state_m)
```

```python
import functools
import math

import jax
import jax.numpy as jnp
from jax import lax
from jax.experimental import pallas as pl
from jax.experimental.pallas import tpu as pltpu

BF = jnp.bfloat16
F32 = jnp.float32

EPS = 1e-6
ROPE_THETA = 10000.0
GRID_W = 64
CHUNK = 128
N_SUB = 3
A_GROUPS = 4
A_DIM = 128
A_WIDTH = A_GROUPS * A_DIM
B_HEADS = 8
B_KV = 2
HEAD_DIM = 64
B_WIDTH = B_HEADS * HEAD_DIM
C_HEADS = 4
C_DK = 128
C_DV = 128
C_WIDTH = C_HEADS * C_DV
N_DIR = 2
N_STREAMS = N_DIR * C_HEADS
D_GROUPS = 4
D_DIM = 128
D_WIDTH = D_GROUPS * D_DIM
LANES = 128

V7X_VMEM_BYTES = 64 * 1024 * 1024
VMEM_LIMIT = V7X_VMEM_BYTES - 4 * 1024 * 1024
ROW_TILE = 512
MLSTM_ROW_TILE = 1024
FFN_ROW_TILE = 1024


def _params(*sem):
    return pltpu.CompilerParams(dimension_semantics=sem, vmem_limit_bytes=VMEM_LIMIT)


def _dot(a, b):
    return jnp.dot(a, b, preferred_element_type=F32)


def _dot_nt(a, b):
    return lax.dot_general(a, b, (((1,), (1,)), ((), ())), preferred_element_type=F32)


def _dot_tn(a, b):
    return lax.dot_general(a, b, (((0,), (0,)), ((), ())), preferred_element_type=F32)


def _sigmoid(x):
    return 1.0 / (1.0 + jnp.exp(-x))


def _gelu_tanh(x):
    return 0.5 * x * (1.0 + jnp.tanh(math.sqrt(2.0 / math.pi) * (x + 0.044715 * (x * x * x))))


def _norm_mod(x, gain, shift, scale):
    ms = jnp.mean(x * x, axis=-1, keepdims=True)
    y = x * lax.rsqrt(ms + EPS) * gain
    return y * (1.0 + scale) + shift


def _const_spec(shape):
    zeros = (0,) * len(shape)
    return pl.BlockSpec(shape, lambda *_: zeros, pipeline_mode=pl.Buffered(1))


def _row_tile(t, target):
    tm = min(t, target)
    assert t % tm == 0
    return tm


def _mod_kernel(c_ref, w_ref, b_ref, o_ref):
    c = c_ref[...]
    sc = (c * _sigmoid(c)).astype(BF)
    o_ref[0] = _dot(sc, w_ref[0].astype(BF)) + b_ref[0]


def _mod_call(c_rows, w_mod, b_mod):
    depth, d, n = w_mod.shape
    rows = c_rows.shape[0]
    tn = 1024
    assert n % tn == 0
    return pl.pallas_call(
        _mod_kernel,
        grid=(depth, n // tn),
        in_specs=[
            pl.BlockSpec((rows, d), lambda l, j: (0, 0)),
            pl.BlockSpec((1, d, tn), lambda l, j: (l, 0, j)),
            pl.BlockSpec((1, 1, tn), lambda l, j: (l, 0, j)),
        ],
        out_specs=pl.BlockSpec((1, rows, tn), lambda l, j: (l, 0, j)),
        out_shape=jax.ShapeDtypeStruct((depth, rows, n), F32),
        compiler_params=_params("parallel", "parallel"),
        name="mod",
    )(c_rows, w_mod, b_mod.reshape(depth, 1, n))


FFN_CHUNK = 256


def _ffn_kernel(*refs, n_mix_a, sub):
    if n_mix_a:
        x_ref = refs[0]
        a_refs = refs[1:1 + n_mix_a]
        b_ref, mod_ref, gain_ref, wa_ref, wb_ref, win_ref, wout_ref, o_ref, act_ref = refs[1 + n_mix_a:]
    else:
        x_ref, mod_ref, gain_ref, win_ref, wout_ref, o_ref, act_ref = refs
    x = x_ref[0]
    if n_mix_a:
        if n_mix_a == 1:
            a = a_refs[0][0]
        else:
            a = (a_refs[0][0].astype(F32) + a_refs[1][0].astype(F32)).astype(BF)
        mix = _dot(a, wa_ref[...]) + _dot(b_ref[0], wb_ref[...])
        x = x + mod_ref[0, 5:6, :] * mix
    h = _norm_mod(x, gain_ref[sub:sub + 1, :], mod_ref[0, 3 * sub:3 * sub + 1, :],
                  mod_ref[0, 3 * sub + 1:3 * sub + 2, :]).astype(BF)
    d_ff = act_ref.shape[1]
    for c in range(d_ff // FFN_CHUNK):
        lo = c * FFN_CHUNK
        g = _dot(h, win_ref[:, lo:lo + FFN_CHUNK])
        u = _dot(h, win_ref[:, d_ff + lo:d_ff + lo + FFN_CHUNK])
        act_ref[:, lo:lo + FFN_CHUNK] = (g * _sigmoid(g) * u).astype(BF)
    y = _dot(act_ref[...], wout_ref[...])
    o_ref[0] = x + (0.5 * mod_ref[0, 3 * sub + 2:3 * sub + 3, :]) * y


def _ffn_call(x, mod, gain, w_in, w_out, layer, slot, sub, mix=None):
    bm, t, d = x.shape
    d_ff = w_out.shape[2]
    weight = lambda w: pl.BlockSpec((None, None) + w.shape[2:], lambda b, i: (layer, slot, 0, 0),
                                    pipeline_mode=pl.Buffered(1))
    tm = _row_tile(t, FFN_ROW_TILE)
    row = lambda w: pl.BlockSpec((1, tm, w), lambda b, i: (b, i, 0))
    in_specs = [row(d)]
    args = [x]
    n_mix_a = 0
    if mix is not None:
        a_list, b_arr, w_a, w_b = mix
        n_mix_a = len(a_list)
        for a in a_list:
            in_specs.append(row(a.shape[-1]))
            args.append(a)
        in_specs.append(row(b_arr.shape[-1]))
        args.append(b_arr)
    in_specs += [pl.BlockSpec((1, 3 * N_SUB, d), lambda b, i: (b, 0, 0)), _const_spec(gain.shape)]
    args += [mod, gain]
    if mix is not None:
        in_specs += [_const_spec(w_a.shape), _const_spec(w_b.shape)]
        args += [w_a, w_b]
    in_specs += [weight(w_in), weight(w_out)]
    args += [w_in, w_out]
    return pl.pallas_call(
        functools.partial(_ffn_kernel, n_mix_a=n_mix_a, sub=sub),
        grid=(bm, t // tm),
        in_specs=in_specs,
        out_specs=row(d),
        out_shape=jax.ShapeDtypeStruct((bm, t, d), F32),
        scratch_shapes=[pltpu.VMEM((tm, d_ff), BF)],
        compiler_params=_params("parallel", "parallel"),
        name="ffn_mix" if mix is not None else "ffn",
    )(*args)


Q_SCALE = HEAD_DIM ** -0.5 * math.log2(math.e)


def _rope(x, cos, sin_signed):
    w = x.shape[1]
    n = w // LANES
    cos = jnp.concatenate([cos] * n, axis=1)
    sin_signed = jnp.concatenate([sin_signed] * n, axis=1)
    lane = lax.broadcasted_iota(jnp.int32, (1, w), 1)
    first_half = (lane & (HEAD_DIM - 1)) < (HEAD_DIM // 2)
    fwd = pltpu.roll(x, w - HEAD_DIM // 2, axis=1)
    bwd = pltpu.roll(x, HEAD_DIM // 2, axis=1)
    return x * cos + jnp.where(first_half, fwd, bwd) * sin_signed


def _even_in_kernel(*refs, has_rope, is_ctx):
    it = iter(refs)
    x_ref, mod_ref, gain_ref, w_ref, sgu_ref, ws_ref, bs_ref, qg_ref, kg_ref, bd_ref = (next(it) for _ in range(10))
    if has_rope:
        cos_ref, sin_ref = next(it), next(it)
    oa_ref, q_ref, kd_ref, vd_ref = (next(it) for _ in range(4))
    if is_ctx:
        k32_ref, v32_ref = next(it), next(it)

    tm = x_ref.shape[1]
    nck = tm // CHUNK
    x = x_ref[0]
    h = _norm_mod(x, gain_ref[1:2, :], mod_ref[0, 3:4, :], mod_ref[0, 4:5, :]).astype(BF)

    off_q = 2 * A_WIDTH
    kvw = 2 * B_KV * HEAD_DIM
    off_k = off_q + B_WIDTH
    off_v = off_k + kvw
    au = _dot(h, w_ref[:, 0:A_WIDTH])
    av = _dot(h, w_ref[:, A_WIDTH:2 * A_WIDTH])
    q = _dot(h, w_ref[:, off_q:off_q + B_WIDTH])
    k = _dot(h, w_ref[:, off_k:off_k + kvw])
    v = _dot(h, w_ref[:, off_v:off_v + kvw])
    q_ms = _dot((q * q).astype(BF), bd_ref[...])
    k_ms = _dot((k * k).astype(BF), bd_ref[0:kvw, 0:kvw])

    av = _gelu_tanh(av)
    gates = []
    for g in range(A_GROUPS):
        cols = slice(g * A_DIM, (g + 1) * A_DIM)
        blk = av[:, cols]
        ms = jnp.mean(blk * blk, axis=-1, keepdims=True)
        vg = (blk * lax.rsqrt(ms + EPS) * sgu_ref[:, cols]).astype(BF)
        rhs = jnp.concatenate([vg[c * CHUNK:(c + 1) * CHUNK, :] for c in range(nck)], axis=1)
        gates.append(_dot(ws_ref[g], rhs))

    lane = lax.broadcasted_iota(jnp.int32, (1, LANES), 1)
    lo_half = lane < HEAD_DIM
    if has_rope:
        cos = cos_ref[...]
        sin = sin_ref[...]
    q = q * lax.rsqrt(q_ms + EPS) * qg_ref[...]
    if has_rope:
        q = _rope(q, cos, sin)
    q_ref[0] = (q * Q_SCALE).astype(BF)
    k = k * lax.rsqrt(k_ms + EPS) * kg_ref[...]
    if is_ctx:
        k32_ref[0] = jnp.where(lo_half, k[:, 0:LANES], k[:, LANES:2 * LANES])
        v32_ref[0] = jnp.where(lo_half, v[:, 0:LANES], v[:, LANES:2 * LANES])
    if has_rope:
        k = _rope(k, cos, sin)
    kd_ref[0] = k.astype(BF)
    spare = (lax.broadcasted_iota(jnp.int32, (1, kvw), 1) & (LANES - 1)) >= HEAD_DIM
    vd_ref[0] = jnp.where(spare, 1.0, v).astype(BF)
    au = _gelu_tanh(au)
    for g in range(A_GROUPS):
        cols = slice(g * A_DIM, (g + 1) * A_DIM)
        for c in range(nck):
            rows = slice(c * CHUNK, (c + 1) * CHUNK)
            gate = gates[g][:, c * A_DIM:(c + 1) * A_DIM] + bs_ref[g]
            oa_ref[0, rows, cols] = (au[rows, cols] * gate).astype(BF)


def _even_in_call(x, mod, gain, w, sgu, ws, bs, qg, kg, bd, rope, is_ctx):
    bm, t, d = x.shape
    tm = _row_tile(t, ROW_TILE)
    kvw = 2 * B_KV * HEAD_DIM
    row = lambda w_: pl.BlockSpec((1, tm, w_), lambda b, i: (b, i, 0))
    in_specs = [row(d), pl.BlockSpec((1, 3 * N_SUB, d), lambda b, i: (b, 0, 0))]
    in_specs += [_const_spec(a.shape) for a in (gain, w, sgu, ws, bs, qg, kg, bd)]
    args = [x, mod, gain, w, sgu, ws, bs, qg, kg, bd]
    if rope is not None:
        in_specs += [pl.BlockSpec((tm, LANES), lambda b, i: (i, 0))] * 2
        args += list(rope)
    out_specs = [row(A_WIDTH), row(B_WIDTH), row(kvw), row(kvw)]
    out_shape = [jax.ShapeDtypeStruct((bm, t, A_WIDTH), BF), jax.ShapeDtypeStruct((bm, t, B_WIDTH), BF),
                 jax.ShapeDtypeStruct((bm, t, kvw), BF), jax.ShapeDtypeStruct((bm, t, kvw), BF)]
    if is_ctx:
        out_specs += [row(B_KV * HEAD_DIM)] * 2
        out_shape += [jax.ShapeDtypeStruct((bm, t, B_KV * HEAD_DIM), F32)] * 2
    return pl.pallas_call(
        functools.partial(_even_in_kernel, has_rope=rope is not None, is_ctx=is_ctx),
        grid=(bm, t // tm),
        in_specs=in_specs,
        out_specs=out_specs,
        out_shape=out_shape,
        compiler_params=_params("parallel", "parallel"),
        name="even_in",
    )(*args)


ATTN_ROWS = 128
ATTN_SHORT_KEYS = 1024


def _attn_kernel(*refs, n_kv):
    q_ref = refs[0]
    k_refs = refs[1:1 + n_kv]
    v_refs = refs[1 + n_kv:1 + 2 * n_kv]
    o_ref = refs[1 + 2 * n_kv]
    if n_kv > 1:
        k_all, v_all = refs[2 + 2 * n_kv:]

        @pl.when(pl.program_id(1) == 0)
        def _():
            off = 0
            for k_ref, v_ref in zip(k_refs, v_refs):
                n = k_ref.shape[1]
                k_all[off:off + n, :] = k_ref[0]
                v_all[off:off + n, :] = v_ref[0]
                off += n

        n_keys = k_all.shape[0]
        keys = lambda cols: k_all[:, cols]
        vals = lambda cols: v_all[:, cols]
    else:
        n_keys = k_refs[0].shape[1]
        keys = lambda cols: k_refs[0][0, :, cols]
        vals = lambda cols: v_refs[0][0, :, cols]
    tq = q_ref.shape[1]
    lane = lax.broadcasted_iota(jnp.int32, (1, LANES), 1)
    lo_half = lane < HEAD_DIM
    heads_per_kv = B_HEADS // B_KV
    units = [(r, h) for r in range(tq // ATTN_ROWS) for h in range(B_HEADS)]

    def kv_cols(h):
        kv = h // heads_per_kv
        return slice(kv * LANES, (kv + 1) * LANES)

    def scores(unit):
        r, h = unit
        p = h // 2
        qp = q_ref[0, r * ATTN_ROWS:(r + 1) * ATTN_ROWS, p * LANES:(p + 1) * LANES]
        keep = lo_half if h % 2 == 0 else jnp.logical_not(lo_half)
        return _dot_nt(jnp.where(keep, qp, jnp.zeros_like(qp)), keys(kv_cols(h)))

    def weights(s):
        return jnp.exp2(s - jnp.max(s, axis=-1, keepdims=True)).astype(BF)

    def attend(e, unit):
        o2 = _dot(e, vals(kv_cols(unit[1])))
        return o2 / pltpu.roll(o2, HEAD_DIM, axis=1)

    outs = {}
    if n_keys <= ATTN_SHORT_KEYS:
        es = [weights(s) for s in [scores(u) for u in units]]
        for e, unit in zip(es, units):
            outs[unit] = attend(e, unit)
    else:
        s_next = scores(units[0])
        pending = None
        for i, unit in enumerate(units):
            s = s_next
            if i + 1 < len(units):
                s_next = scores(units[i + 1])
            e = weights(s)
            if pending is not None:
                outs[pending[1]] = attend(*pending)
            pending = (e, unit)
        outs[pending[1]] = attend(*pending)
    for r in range(tq // ATTN_ROWS):
        for p in range(B_WIDTH // LANES):
            o_ref[0, r * ATTN_ROWS:(r + 1) * ATTN_ROWS, p * LANES:(p + 1) * LANES] = jnp.where(
                lo_half, outs[(r, 2 * p)], pltpu.roll(outs[(r, 2 * p + 1)], HEAD_DIM, axis=1)).astype(BF)


def _attn_call(q, k_blocks, v_blocks):
    b, t, w = q.shape
    tq = _row_tile(t, ROW_TILE)
    whole = lambda a: pl.BlockSpec((1,) + a.shape[1:], lambda i, j: (i, 0, 0))
    n_kv = len(k_blocks)
    scratch = []
    if n_kv > 1:
        s_all = sum(a.shape[1] for a in k_blocks)
        scratch = [pltpu.VMEM((s_all, k_blocks[0].shape[2]), BF), pltpu.VMEM((s_all, v_blocks[0].shape[2]), BF)]
    return pl.pallas_call(
        functools.partial(_attn_kernel, n_kv=n_kv),
        grid=(b, t // tq),
        in_specs=[pl.BlockSpec((1, tq, w), lambda i, j: (i, j, 0))]
        + [whole(a) for a in k_blocks] + [whole(a) for a in v_blocks],
        out_specs=pl.BlockSpec((1, tq, w), lambda i, j: (i, j, 0)),
        out_shape=jax.ShapeDtypeStruct((b, t, w), BF),
        scratch_shapes=scratch,
        compiler_params=_params("parallel", "arbitrary"),
        name="attn",
    )(q, *k_blocks, *v_blocks)


N_GATE_ROWS = 2 * N_STREAMS


def _log_sigmoid(x):
    return jnp.minimum(x, 0.0) - jnp.log(1.0 + jnp.exp(-jnp.abs(x)))


def _odd_in_kernel(x_ref, mod_ref, gain_ref, w_ref, wt_ref, wgt_ref, bg_ref,
                   q_ref, k_ref, d_ref, vt_ref, ot_ref, gt_ref):
    tm = x_ref.shape[1]
    x = x_ref[0]
    h = _norm_mod(x, gain_ref[1:2, :], mod_ref[0, 3:4, :], mod_ref[0, 4:5, :]).astype(BF)
    w = C_WIDTH
    q_ref[0] = _dot(h, w_ref[:, 0:w]).astype(BF)
    k_ref[0] = (_dot(h, w_ref[:, w:2 * w]) * (C_DK ** -0.5)).astype(BF)
    d_ref[0] = _dot(h, w_ref[:, 2 * w:2 * w + D_WIDTH]).astype(BF)
    vt = _dot_nt(wt_ref[0:w, :], h)
    ot = _sigmoid(_dot_nt(wt_ref[w:2 * w, :], h))
    gt = _dot_nt(wgt_ref[...], h)
    row = lax.broadcasted_iota(jnp.int32, (N_GATE_ROWS, CHUNK), 0)
    for c in range(tm // CHUNK):
        lanes = slice(c * CHUNK, (c + 1) * CHUNK)
        vt_ref[0, c] = vt[:, lanes].astype(BF)
        ot_ref[0, c] = ot[:, lanes].astype(BF)
        g = gt[:, lanes] + bg_ref[...]
        gt_ref[0, c] = jnp.where(row >= N_STREAMS, _log_sigmoid(g), g)


def _odd_in_call(x, mod, gain, w, wt, wgt, bg):
    bm, t, d = x.shape
    tm = _row_tile(t, ROW_TILE)
    nck = tm // CHUNK
    row = lambda w_: pl.BlockSpec((1, tm, w_), lambda b, i: (b, i, 0))
    chunked = lambda r: pl.BlockSpec((1, nck, r, CHUNK), lambda b, i: (b, i, 0, 0))
    return pl.pallas_call(
        _odd_in_kernel,
        grid=(bm, t // tm),
        in_specs=[row(d), pl.BlockSpec((1, 3 * N_SUB, d), lambda b, i: (b, 0, 0))]
        + [_const_spec(a.shape) for a in (gain, w, wt, wgt, bg)],
        out_specs=[row(C_WIDTH), row(C_WIDTH), row(D_WIDTH), chunked(C_WIDTH), chunked(C_WIDTH),
                   chunked(N_GATE_ROWS)],
        out_shape=[jax.ShapeDtypeStruct((bm, t, C_WIDTH), BF), jax.ShapeDtypeStruct((bm, t, C_WIDTH), BF),
                   jax.ShapeDtypeStruct((bm, t, D_WIDTH), BF),
                   jax.ShapeDtypeStruct((bm, t // CHUNK, C_WIDTH, CHUNK), BF),
                   jax.ShapeDtypeStruct((bm, t // CHUNK, C_WIDTH, CHUNK), BF),
                   jax.ShapeDtypeStruct((bm, t // CHUNK, N_GATE_ROWS, CHUNK), F32)],
        compiler_params=_params("parallel", "parallel"),
        name="odd_in",
    )(x, mod, gain, w, wt, wgt, bg)


N_REP = 8
CN_ROWS = C_DV + N_REP


def _split3(x):
    hi = x.astype(BF)
    r1 = x - hi.astype(F32)
    mid = r1.astype(BF)
    lo = (r1 - mid.astype(F32)).astype(BF)
    return hi, mid, lo


def _mlstm_kernel(qf_ref, kf_ref, vtf_ref, otf_ref, gtf_ref, qb_ref, kb_ref, vtb_ref, otb_ref, gtb_ref,
                  c0_ref, n0_ref, m0_ref,
                  hf_ref, hb_ref, c1_ref, n1_ref, m1_ref,
                  cn_s, m_s):
    i = pl.program_id(1)
    nck = qf_ref.shape[1] // CHUNK

    @pl.when(i == 0)
    def _():
        for sl in range(N_STREAMS):
            cn_s[sl, 0:C_DV, :] = c0_ref[0, sl]
            cn_s[sl, C_DV:CN_ROWS, :] = jnp.broadcast_to(n0_ref[0, sl:sl + 1, :], (N_REP, C_DK))
        m_s[...] = m0_ref[0]

    r_i = lax.broadcasted_iota(jnp.int32, (CHUNK, CHUNK), 0)
    c_i = lax.broadcasted_iota(jnp.int32, (CHUNK, CHUNK), 1)
    upper = c_i >= r_i
    lower = c_i <= r_i
    tri_up = jnp.where(upper, 1.0, 0.0).astype(BF)
    tri_lo = jnp.where(lower, 1.0, 0.0).astype(BF)
    eye = jnp.where(c_i == r_i, 1.0, 0.0).astype(BF)
    ones = jnp.ones((CHUNK, CHUNK), BF)
    n_rows = nck * N_STREAMS
    assert n_rows <= CHUNK
    is_fwd = (lax.broadcasted_iota(jnp.int32, (n_rows, CHUNK), 0) & (N_STREAMS - 1)) < C_HEADS
    is_fwd8 = is_fwd[0:N_STREAMS]
    lane = lax.broadcasted_iota(jnp.int32, (n_rows, CHUNK), 1)

    g_i, g_f = [], []
    for c in range(nck):
        gf = gtf_ref[0, c]
        gb = gtb_ref[0, nck - 1 - c]
        g_i.append(jnp.where(is_fwd8, gf[0:N_STREAMS], gb[0:N_STREAMS]))
        g_f.append(jnp.where(is_fwd8, gf[N_STREAMS:N_GATE_ROWS], gb[N_STREAMS:N_GATE_ROWS]))
    g_i = jnp.concatenate(g_i, axis=0)
    g_f = jnp.concatenate(g_f, axis=0)
    parts = _split3(g_f)
    cum = jnp.where(is_fwd, sum(_dot(p, tri_up) for p in parts),
                    sum(_dot(p, tri_lo) for p in parts))
    total = sum(_dot(p, ones) for p in parts)
    r = g_i - cum
    run = r
    sh = 1
    while sh < CHUNK:
        prev = jnp.where(lane >= sh, pltpu.roll(run, sh, axis=1), -jnp.inf)
        nxt = jnp.where(lane < CHUNK - sh, pltpu.roll(run, CHUNK - sh, axis=1), -jnp.inf)
        run = jnp.maximum(run, jnp.where(is_fwd, prev, nxt))
        sh *= 2
    log_s = total - cum + g_i
    ls_max = jnp.max(log_s, axis=-1, keepdims=True)
    m = m_s[...]
    m_old, m_new = [], []
    for c in range(nck):
        rows = slice(c * N_STREAMS, (c + 1) * N_STREAMS)
        m_old.append(m)
        m = jnp.maximum(total[rows] + m, ls_max[rows])
        m_new.append(m)
    m_s[...] = m
    m_old = jnp.concatenate(m_old, axis=0)
    m_new = jnp.concatenate(m_new, axis=0)
    log_inter = cum + m_old
    m_t = jnp.maximum(log_inter, cum + run)
    bm = cum - m_t
    w_prev = jnp.exp(log_inter - m_t)
    e_inv = jnp.exp(-m_t)
    w_src = jnp.exp(log_s - m_new)
    w_keep = jnp.exp(total + m_old - m_new)
    r_cols = jnp.concatenate([r, jnp.zeros((CHUNK - n_rows, CHUNK), F32)], axis=0).T

    row = lambda a, j: a[j:j + 1, :]
    units = []
    for c in range(nck):
        for sl in range(N_STREAMS):
            fwd = sl < C_HEADS
            q_ref, k_ref, vt_ref, ot_ref, h_ref = ((qf_ref, kf_ref, vtf_ref, otf_ref, hf_ref) if fwd
                                                   else (qb_ref, kb_ref, vtb_ref, otb_ref, hb_ref))
            ck = c if fwd else nck - 1 - c
            rows = slice(ck * CHUNK, (ck + 1) * CHUNK)
            cols = slice((sl % C_HEADS) * C_DK, (sl % C_HEADS + 1) * C_DK)
            units.append((c * N_STREAMS + sl, sl, q_ref[0, rows, cols], k_ref[0, rows, cols],
                          vt_ref[0, ck, cols, :], ot_ref[0, ck, cols, :], h_ref, rows, cols))
    s_t = [_dot_nt(k, q) for _, _, q, k, _, _, _, _, _ in units]
    upd = []
    for j, _, _, k, vt, _, _, _, _ in units:
        ws = row(w_src, j)
        lhs = jnp.concatenate([(vt.astype(F32) * ws).astype(BF),
                               jnp.broadcast_to(ws, (N_REP, CHUNK)).astype(BF)], axis=0)
        upd.append(_dot(lhs, k))
    a_t = []
    for (j, sl, *_), s in zip(units, s_t):
        mask = upper if sl < C_HEADS else lower
        a_t.append(jnp.exp(jnp.where(mask, r_cols[:, j:j + 1] + row(bm, j), -jnp.inf)) * s)
    num = [_dot(u[4], a.astype(BF)) for u, a in zip(units, a_t)]
    cn = [cn_s[sl] for sl in range(N_STREAMS)]
    step_units = lambda c: range(c * N_STREAMS, (c + 1) * N_STREAMS)

    def read_state(c):
        out = []
        for idx in step_units(c):
            j, sl, q = units[idx][0:3]
            out.append(_dot_nt(cn[sl].astype(BF), q))
            cn[sl] = row(w_keep, j) * cn[sl] + upd[idx]
        return out

    inter = read_state(0)
    for c in range(nck):
        h_t = []
        for idx, it in zip(step_units(c), inter):
            j, ot = units[idx][0], units[idx][5]
            wp = row(w_prev, j)
            den = jnp.sum(a_t[idx], axis=0, keepdims=True) + wp * it[C_DV:C_DV + 1]
            denom = jnp.maximum(jnp.abs(den), row(e_inv, j))
            h_t.append(((num[idx] + wp * it[0:C_DV]) * (1.0 / denom) * ot.astype(F32)).astype(BF))
        if c + 1 < nck:
            inter = read_state(c + 1)
        h_rows = [_dot_nt(eye, h).astype(BF) for h in h_t]
        for idx, h in zip(step_units(c), h_rows):
            h_ref, rows, cols = units[idx][6:9]
            h_ref[0, rows, cols] = h
    for sl in range(N_STREAMS):
        cn_s[sl] = cn[sl]

    @pl.when(i == pl.num_programs(1) - 1)
    def _():
        for sl in range(N_STREAMS):
            c1_ref[0, sl] = cn_s[sl, 0:C_DV, :]
            n1_ref[0, sl:sl + 1, :] = cn_s[sl, C_DV:C_DV + 1, :]
        m1_ref[0] = m_s[...]


def _mlstm_call(q, k, vt, ot, gt, c0, n0, m0):
    b, t, w = q.shape
    tb = _row_tile(t, MLSTM_ROW_TILE)
    nt = t // tb
    nck = tb // CHUNK
    fwd = pl.BlockSpec((1, tb, w), lambda i, j: (i, j, 0))
    bwd = pl.BlockSpec((1, tb, w), lambda i, j: (i, nt - 1 - j, 0))
    fwd_t = lambda r: pl.BlockSpec((1, nck, r, CHUNK), lambda i, j: (i, j, 0, 0))
    bwd_t = lambda r: pl.BlockSpec((1, nck, r, CHUNK), lambda i, j: (i, nt - 1 - j, 0, 0))
    st_c = pl.BlockSpec((1, N_STREAMS, C_DV, C_DK), lambda i, j: (i, 0, 0, 0))
    st_n = pl.BlockSpec((1, N_STREAMS, C_DK), lambda i, j: (i, 0, 0))
    st_m = pl.BlockSpec((1, N_STREAMS, CHUNK), lambda i, j: (i, 0, 0))
    return pl.pallas_call(
        _mlstm_kernel,
        grid=(b, nt),
        in_specs=[fwd, fwd, fwd_t(w), fwd_t(w), fwd_t(N_GATE_ROWS), bwd, bwd, bwd_t(w), bwd_t(w),
                  bwd_t(N_GATE_ROWS), st_c, st_n, st_m],
        out_specs=[fwd, bwd, st_c, st_n, st_m],
        out_shape=[jax.ShapeDtypeStruct((b, t, w), BF), jax.ShapeDtypeStruct((b, t, w), BF),
                   jax.ShapeDtypeStruct(c0.shape, F32), jax.ShapeDtypeStruct(n0.shape, F32),
                   jax.ShapeDtypeStruct(m0.shape, F32)],
        scratch_shapes=[pltpu.VMEM((N_STREAMS, CN_ROWS, C_DK), F32), pltpu.VMEM((N_STREAMS, CHUNK), F32)],
        compiler_params=_params("parallel", "arbitrary"),
        name="mlstm",
    )(q, k, vt, ot, gt, q, k, vt, ot, gt, c0, n0, m0)


def _fourier_kernel(d_ref, dft_t_ref, dft_c_ref, o_ref, y_s, *, scale):
    t = d_ref.shape[1]

    @pl.when(pl.program_id(1) == 0)
    def _():
        step = min(t, 512)
        for r in range(t // step):
            rows = slice(r * step, (r + 1) * step)
            for g in range(D_GROUPS):
                cols = slice(g * D_DIM, (g + 1) * D_DIM)
                y = _dot(d_ref[0, rows, cols], dft_c_ref[...])
                y_s[rows, cols] = y[:, 0:D_DIM].astype(BF)
                y_s[t + r * step:t + (r + 1) * step, cols] = y[:, D_DIM:2 * D_DIM].astype(BF)

    o_ref[0] = (_dot(dft_t_ref[...], y_s[...]) * scale).astype(BF)


def _fourier_call(d, dft_t, dft_c):
    b, t, w = d.shape
    tr = _row_tile(t, ROW_TILE)
    return pl.pallas_call(
        functools.partial(_fourier_kernel, scale=float((t * D_DIM) ** -0.5)),
        grid=(b, t // tr),
        in_specs=[
            pl.BlockSpec((1, t, w), lambda i, j: (i, 0, 0)),
            pl.BlockSpec((tr, 2 * t), lambda i, j: (j, 0)),
            _const_spec(dft_c.shape),
        ],
        out_specs=pl.BlockSpec((1, tr, w), lambda i, j: (i, j, 0)),
        out_shape=jax.ShapeDtypeStruct((b, t, w), BF),
        scratch_shapes=[pltpu.VMEM((2 * t, w), BF)],
        compiler_params=_params("parallel", "arbitrary"),
        name="fourier",
    )(d, dft_t, dft_c)


DFT_COLS = 16
DFT_RESIDUES = 8


def _dft_rows_kernel(x_ref, f_ref, zr_ref, zi_ref):
    n1, nc, w = x_ref.shape[1:]
    z = _dot(f_ref[...], x_ref[0].reshape(n1 * nc, w))
    zr_ref[0] = z[0:n1 * nc].astype(BF).reshape(n1, nc, w)
    zi_ref[0] = z[n1 * nc:2 * n1 * nc].astype(BF).reshape(n1, nc, w)


def _dft_rows_call(x4, f1):
    b, n1, n2, w = x4.shape
    spec = pl.BlockSpec((1, n1, DFT_COLS, w), lambda i, j: (i, 0, j, 0))
    return pl.pallas_call(
        _dft_rows_kernel,
        grid=(b, n2 // DFT_COLS),
        in_specs=[spec, _const_spec(f1.shape)],
        out_specs=[spec, spec],
        out_shape=[jax.ShapeDtypeStruct(x4.shape, BF)] * 2,
        compiler_params=_params("parallel", "parallel"),
        name="dft_rows",
    )(x4, f1)


def _dft_cols_kernel(zr_ref, zi_ref, g_ref, c_ref, o_ref, *, scale):
    n2 = GRID_W
    nb = zr_ref.shape[1]
    w = zr_ref.shape[3]
    xr, xi = [], []
    for r in range(nb):
        z = jnp.concatenate([zr_ref[0, r], zi_ref[0, r]], axis=0)
        x = _dot(g_ref[r], z)
        xr.append(x[0:n2])
        xi.append(x[n2:2 * n2])
    xr = jnp.concatenate(xr, axis=0).astype(BF)
    xi = jnp.concatenate(xi, axis=0).astype(BF)
    for g in range(D_GROUPS):
        cols = slice(g * D_DIM, (g + 1) * D_DIM)
        y = _dot(jnp.concatenate([xr[:, cols], xi[:, cols]], axis=1), c_ref[...]) * scale
        for r in range(nb):
            o_ref[0, :, r * w + g * D_DIM:r * w + (g + 1) * D_DIM] = y[r * n2:(r + 1) * n2].astype(BF)


def _dft_cols_call(zr, zi, gmat, cmat, scale):
    b, n1, n2, w = zr.shape
    nb = DFT_RESIDUES
    zspec = pl.BlockSpec((1, nb, n2, w), lambda i, j: (i, j, 0, 0))
    return pl.pallas_call(
        functools.partial(_dft_cols_kernel, scale=scale),
        grid=(b, n1 // nb),
        in_specs=[zspec, zspec, pl.BlockSpec((nb, 2 * n2, 2 * n2), lambda i, j: (j, 0, 0)),
                  _const_spec(cmat.shape)],
        out_specs=pl.BlockSpec((1, n2, nb * w), lambda i, j: (i, 0, j)),
        out_shape=jax.ShapeDtypeStruct((b, n2, n1 * w), BF),
        compiler_params=_params("parallel", "parallel"),
        name="dft_cols",
    )(zr, zi, gmat, cmat)


def _cos_sin(num, den):
    ang = (num % den).astype(F32) * (2.0 * math.pi / den)
    return jnp.cos(ang), jnp.sin(ang)


def _dft_cos_sin(n):
    k = jnp.arange(n, dtype=jnp.int32)
    return _cos_sin(k[:, None] * k[None, :], n)


DENSE_DFT_MAX = 1024


def _fourier(d):
    b, t, w = d.shape
    cc, sc = _dft_cos_sin(D_DIM)
    scale = float((t * D_DIM) ** -0.5)
    if t <= DENSE_DFT_MAX:
        ct, st = _dft_cos_sin(t)
        return _fourier_call(d, jnp.concatenate([ct, -st], axis=1).astype(BF),
                             jnp.concatenate([cc, sc], axis=1).astype(BF))
    n2 = GRID_W
    n1 = t // n2
    assert n1 * n2 == t and n1 % DFT_RESIDUES == 0 and n2 % DFT_COLS == 0
    a1 = jnp.arange(n1, dtype=jnp.int32)
    a2 = jnp.arange(n2, dtype=jnp.int32)
    c1, s1 = _cos_sin(a1[:, None] * a1[None, :], n1)
    idx = jnp.arange(n1 * DFT_COLS, dtype=jnp.int32)
    rep = (idx[:, None] // DFT_COLS == a1[None, :]).astype(BF)
    same_col = (idx[:, None] % DFT_COLS) == (idx[None, :] % DFT_COLS)
    expand = lambda f: jnp.where(same_col, _dot(_dot(rep, f.astype(BF)).astype(BF), rep.T), 0.0)
    f1 = jnp.concatenate([expand(c1), expand(-s1)], axis=0).astype(BF)
    tp = a1[:, None, None] + n1 * a2[None, :, None]
    gc, gs = _cos_sin(tp * a2[None, None, :], t)
    gmat = jnp.concatenate([jnp.concatenate([gc, gs], axis=2),
                            jnp.concatenate([-gs, gc], axis=2)], axis=1).astype(BF)
    cmat = jnp.concatenate([cc, sc], axis=0).astype(BF)
    zr, zi = _dft_rows_call(d.reshape(b, n1, n2, w), f1)
    out = _dft_cols_call(zr, zi, gmat, cmat, scale)
    return out.reshape(b, t, w)


def _rope_tables(t):
    rows = t // GRID_W
    r = jnp.repeat(jnp.arange(rows), GRID_W).astype(F32)
    cidx = jnp.tile(jnp.arange(GRID_W), rows).astype(F32)
    n_freq = HEAD_DIM // 4
    inv = ROPE_THETA ** (-jnp.arange(n_freq, dtype=F32) / n_freq)
    ang = jnp.concatenate([r[:, None] * inv, cidx[:, None] * inv], axis=-1)
    ang = jnp.concatenate([ang] * (2 * LANES // HEAD_DIM), axis=-1)
    lane = jnp.arange(LANES)
    sign = jnp.where((lane % HEAD_DIM) < HEAD_DIM // 2, -1.0, 1.0).astype(F32)
    return jnp.cos(ang), jnp.sin(ang) * sign


def _dup_heads(a, ones=False):
    parts = []
    for h in range(B_KV):
        blk = a[..., h * HEAD_DIM:(h + 1) * HEAD_DIM]
        parts += [blk, jnp.ones_like(blk) if ones else blk]
    return jnp.concatenate(parts, axis=-1)


def kernel(x_prompt, x_sample, c, cache_k, cache_v, state_C, state_n, state_m, c_ctx, norm_gain, w_mod, b_mod,
           ffn_w_in, ffn_w_out, w_in_even, w_out_even, spatial_w, spatial_b, sgu_gain, q_gain, k_gain,
           w_in_odd, b_gate_odd, w_out_odd):
    depth = norm_gain.shape[0]
    batch, seq, d_model = x_prompt.shape
    dec_batch, dec_seq, _ = x_sample.shape

    rows = 1 + dec_batch
    rows_pad = -(-rows // 8) * 8
    c_rows = jnp.concatenate([c_ctx[None, :], c, jnp.zeros((rows_pad - rows, d_model), F32)], axis=0)
    mod_all = _mod_call(c_rows, w_mod, b_mod)
    mod_ctx = mod_all[:, 0:1].reshape(depth, 1, 3 * N_SUB, d_model)
    mod_lat = mod_all[:, 1:rows].reshape(depth, dec_batch, 3 * N_SUB, d_model)

    ffn_in = ffn_w_in.astype(BF)
    ffn_out = ffn_w_out.astype(BF)

    kv = B_KV * HEAD_DIM
    lane = jnp.arange(LANES)
    bd = ((jnp.arange(B_WIDTH)[:, None] // HEAD_DIM) == (jnp.arange(B_WIDTH)[None, :] // HEAD_DIM))
    bd = jnp.where(bd, 1.0 / HEAD_DIM, 0.0).astype(BF)

    def even_weights(e):
        w = w_in_even[e]
        o = 2 * A_WIDTH + B_WIDTH
        w_cat = jnp.concatenate([w[:, :o], _dup_heads(w[:, o:o + kv]), _dup_heads(w[:, o + kv:o + 2 * kv])], axis=1)
        bs = jnp.broadcast_to(spatial_b[e][:, :, None], (A_GROUPS, CHUNK, A_DIM)).astype(F32)
        return dict(
            w=w_cat.astype(BF), sgu=sgu_gain[e].reshape(1, A_WIDTH), ws=spatial_w[e].astype(BF), bs=bs,
            qg=jnp.tile(q_gain[e], B_HEADS)[None, :], kg=jnp.tile(k_gain[e], 2 * B_KV)[None, :],
            wo_a=w_out_even[e][:A_WIDTH].astype(BF), wo_b=w_out_even[e][A_WIDTH:].astype(BF))

    def odd_weights(e):
        w = w_in_odd[e]
        o = 2 * C_HEADS * C_DK + 2 * C_WIDTH
        n_g = 2 * N_STREAMS
        qk = 2 * C_HEADS * C_DK
        w_rows = jnp.concatenate([w[:, :qk], w[:, o + n_g:]], axis=1)
        w_t = w[:, qk:o].T
        perm = jnp.array([d * 2 * C_HEADS + g * C_HEADS + h
                          for g in range(2) for d in range(N_DIR) for h in range(C_HEADS)])
        wg_t = w[:, o:o + n_g][:, perm].T
        bg = jnp.broadcast_to(b_gate_odd[e].reshape(-1)[perm][:, None], (n_g, CHUNK)).astype(F32)
        return dict(w=w_rows.astype(BF), wt=w_t.astype(BF), wgt=wg_t.astype(BF), bg=bg,
                    wo_a=w_out_odd[e][:C_WIDTH].astype(BF), wo_b=w_out_odd[e][C_WIDTH:].astype(BF))

    def run_pass(x, mods, b, t, rope, ctx_even, ctx_odd, is_ctx):
        bm = x.shape[0]
        per_seq = lambda a: a.reshape(b, t, a.shape[-1])
        flat = lambda a: a.reshape(bm, -1, a.shape[-1])
        new_even, new_odd = [], []
        for l in range(depth):
            e = l // 2
            mod, gain = mods[l], norm_gain[l]
            x = _ffn_call(x, mod, gain, ffn_in, ffn_out, l, 0, sub=0)
            if l % 2 == 0:
                p = even_weights(e)
                outs = _even_in_call(x, mod, gain, p["w"], p["sgu"], p["ws"], p["bs"], p["qg"], p["kg"], bd,
                                     rope, is_ctx)
                oa, q, kd, vd = outs[:4]
                k_blocks, v_blocks = [per_seq(kd)], [per_seq(vd)]
                if is_ctx:
                    new_even.append((outs[4], outs[5]))
                else:
                    ck, cv = ctx_even[e]
                    k_blocks.insert(0, _dup_heads(ck.reshape(b, -1, kv)).astype(BF))
                    v_blocks.insert(0, _dup_heads(cv.reshape(b, -1, kv), ones=True).astype(BF))
                ob = _attn_call(per_seq(q), k_blocks, v_blocks)
                mix = ([oa], flat(ob), p["wo_a"], p["wo_b"])
            else:
                p = odd_weights(e)
                q, k, dd, vt, ot, gt = _odd_in_call(x, mod, gain, p["w"], p["wt"], p["wgt"], p["bg"])
                per_seq_t = lambda a: a.reshape((b, t // CHUNK) + a.shape[2:])
                c0, n0, m0 = ctx_odd[e]
                hf, hb, c1, n1, m1 = _mlstm_call(per_seq(q), per_seq(k), per_seq_t(vt), per_seq_t(ot),
                                                 per_seq_t(gt), c0, n0, m0)
                new_odd.append((c1, n1, m1))
                od = _fourier(per_seq(dd))
                mix = ([flat(hf), flat(hb)], flat(od), p["wo_a"], p["wo_b"])
            x = _ffn_call(x, mod, gain, ffn_in, ffn_out, l, 1, sub=2, mix=mix)
        return x, new_even, new_odd

    n_even = (depth + 1) // 2
    n_odd = depth // 2

    zero_state = (jnp.zeros((batch, N_STREAMS, C_DV, C_DK), F32), jnp.zeros((batch, N_STREAMS, C_DK), F32),
                  jnp.zeros((batch, N_STREAMS, CHUNK), F32))
    y, new_even, new_odd = run_pass(x_prompt.reshape(1, batch * seq, d_model), list(mod_ctx), batch, seq,
                                    None, None, [zero_state] * n_odd, True)
    y_prompt = y.reshape(batch, seq, d_model)
    new_cache_k = jnp.stack([k.reshape(batch, seq, B_KV, HEAD_DIM) for k, _ in new_even], axis=1)
    new_cache_v = jnp.stack([v.reshape(batch, seq, B_KV, HEAD_DIM) for _, v in new_even], axis=1)
    new_state_c = jnp.stack([c1.reshape(batch, N_DIR, C_HEADS, C_DV, C_DK) for c1, _, _ in new_odd], axis=1)
    new_state_n = jnp.stack([n1.reshape(batch, N_DIR, C_HEADS, C_DK) for _, n1, _ in new_odd], axis=1)
    new_state_m = jnp.stack([m1[:, :, 0].reshape(batch, N_DIR, C_HEADS) for _, _, m1 in new_odd], axis=1)

    ctx_even = [(cache_k[:, e], cache_v[:, e]) for e in range(n_even)]
    ctx_odd = []
    for e in range(n_odd):
        m0 = jnp.broadcast_to(state_m[:, e].reshape(dec_batch, N_STREAMS, 1), (dec_batch, N_STREAMS, CHUNK))
        ctx_odd.append((state_C[:, e].reshape(dec_batch, N_STREAMS, C_DV, C_DK),
                        state_n[:, e].reshape(dec_batch, N_STREAMS, C_DK), m0))
    y_sample, _, _ = run_pass(x_sample, list(mod_lat), dec_batch, dec_seq, _rope_tables(dec_seq),
                              ctx_even, ctx_odd, False)

    return (y_prompt, y_sample, new_cache_k, new_cache_v, new_state_c, new_state_n, new_state_m)
```

```python
import functools
import math

import jax
import jax.numpy as jnp
from jax import lax
from jax.experimental import pallas as pl
from jax.experimental.pallas import tpu as pltpu

BF = jnp.bfloat16
F32 = jnp.float32

EPS = 1e-6
ROPE_THETA = 10000.0
GRID_W = 64
CHUNK = 128
N_SUB = 3
A_GROUPS = 4
A_DIM = 128
A_WIDTH = A_GROUPS * A_DIM
B_HEADS = 8
B_KV = 2
HEAD_DIM = 64
B_WIDTH = B_HEADS * HEAD_DIM
C_HEADS = 4
C_DK = 128
C_DV = 128
C_WIDTH = C_HEADS * C_DV
N_DIR = 2
N_STREAMS = N_DIR * C_HEADS
D_GROUPS = 4
D_DIM = 128
D_WIDTH = D_GROUPS * D_DIM
LANES = 128

V7X_VMEM_BYTES = 64 * 1024 * 1024
VMEM_LIMIT = V7X_VMEM_BYTES - 4 * 1024 * 1024
ROW_TILE = 512
MLSTM_ROW_TILE = 1024
FFN_ROW_TILE = 1024


def _params(*sem):
    return pltpu.CompilerParams(dimension_semantics=sem, vmem_limit_bytes=VMEM_LIMIT)


def _dot(a, b):
    return jnp.dot(a, b, preferred_element_type=F32)


def _dot_nt(a, b):
    return lax.dot_general(a, b, (((1,), (1,)), ((), ())), preferred_element_type=F32)


def _dot_tn(a, b):
    return lax.dot_general(a, b, (((0,), (0,)), ((), ())), preferred_element_type=F32)


def _sigmoid(x):
    return 1.0 / (1.0 + jnp.exp(-x))


def _gelu_tanh(x):
    return 0.5 * x * (1.0 + jnp.tanh(math.sqrt(2.0 / math.pi) * (x + 0.044715 * (x * x * x))))


def _norm_mod(x, gain, shift, scale):
    ms = jnp.mean(x * x, axis=-1, keepdims=True)
    y = x * lax.rsqrt(ms + EPS) * gain
    return y * (1.0 + scale) + shift


def _const_spec(shape):
    zeros = (0,) * len(shape)
    return pl.BlockSpec(shape, lambda *_: zeros, pipeline_mode=pl.Buffered(1))


def _row_tile(t, target):
    tm = min(t, target)
    assert t % tm == 0
    return tm


def _mod_kernel(c_ref, w_ref, b_ref, o_ref):
    c = c_ref[...]
    sc = (c * _sigmoid(c)).astype(BF)
    o_ref[0] = _dot(sc, w_ref[0].astype(BF)) + b_ref[0]


def _mod_call(c_rows, w_mod, b_mod):
    depth, d, n = w_mod.shape
    rows = c_rows.shape[0]
    tn = 1024
    assert n % tn == 0
    return pl.pallas_call(
        _mod_kernel,
        grid=(depth, n // tn),
        in_specs=[
            pl.BlockSpec((rows, d), lambda l, j: (0, 0)),
            pl.BlockSpec((1, d, tn), lambda l, j: (l, 0, j)),
            pl.BlockSpec((1, 1, tn), lambda l, j: (l, 0, j)),
        ],
        out_specs=pl.BlockSpec((1, rows, tn), lambda l, j: (l, 0, j)),
        out_shape=jax.ShapeDtypeStruct((depth, rows, n), F32),
        compiler_params=_params("parallel", "parallel"),
        name="mod",
    )(c_rows, w_mod, b_mod.reshape(depth, 1, n))


FFN_CHUNK = 256


def _ffn_kernel(*refs, n_mix_a, sub):
    if n_mix_a:
        x_ref = refs[0]
        a_refs = refs[1:1 + n_mix_a]
        b_ref, mod_ref, gain_ref, wa_ref, wb_ref, win_ref, wout_ref, o_ref, act_ref = refs[1 + n_mix_a:]
    else:
        x_ref, mod_ref, gain_ref, win_ref, wout_ref, o_ref, act_ref = refs
    x = x_ref[0]
    if n_mix_a:
        if n_mix_a == 1:
            a = a_refs[0][0]
        else:
            a = (a_refs[0][0].astype(F32) + a_refs[1][0].astype(F32)).astype(BF)
        mix = _dot(a, wa_ref[...]) + _dot(b_ref[0], wb_ref[...])
        x = x + mod_ref[0, 5:6, :] * mix
    h = _norm_mod(x, gain_ref[sub:sub + 1, :], mod_ref[0, 3 * sub:3 * sub + 1, :],
                  mod_ref[0, 3 * sub + 1:3 * sub + 2, :]).astype(BF)
    d_ff = act_ref.shape[1]
    for c in range(d_ff // FFN_CHUNK):
        lo = c * FFN_CHUNK
        g = _dot(h, win_ref[:, lo:lo + FFN_CHUNK])
        u = _dot(h, win_ref[:, d_ff + lo:d_ff + lo + FFN_CHUNK])
        act_ref[:, lo:lo + FFN_CHUNK] = (g * _sigmoid(g) * u).astype(BF)
    y = _dot(act_ref[...], wout_ref[...])
    o_ref[0] = x + (0.5 * mod_ref[0, 3 * sub + 2:3 * sub + 3, :]) * y


def _ffn_call(x, mod, gain, w_in, w_out, layer, slot, sub, mix=None):
    bm, t, d = x.shape
    d_ff = w_out.shape[2]
    weight = lambda w: pl.BlockSpec((None, None) + w.shape[2:], lambda b, i: (layer, slot, 0, 0),
                                    pipeline_mode=pl.Buffered(1))
    tm = _row_tile(t, FFN_ROW_TILE)
    row = lambda w: pl.BlockSpec((1, tm, w), lambda b, i: (b, i, 0))
    in_specs = [row(d)]
    args = [x]
    n_mix_a = 0
    if mix is not None:
        a_list, b_arr, w_a, w_b = mix
        n_mix_a = len(a_list)
        for a in a_list:
            in_specs.append(row(a.shape[-1]))
            args.append(a)
        in_specs.append(row(b_arr.shape[-1]))
        args.append(b_arr)
    in_specs += [pl.BlockSpec((1, 3 * N_SUB, d), lambda b, i: (b, 0, 0)), _const_spec(gain.shape)]
    args += [mod, gain]
    if mix is not None:
        in_specs += [_const_spec(w_a.shape), _const_spec(w_b.shape)]
        args += [w_a, w_b]
    in_specs += [weight(w_in), weight(w_out)]
    args += [w_in, w_out]
    return pl.pallas_call(
        functools.partial(_ffn_kernel, n_mix_a=n_mix_a, sub=sub),
        grid=(bm, t // tm),
        in_specs=in_specs,
        out_specs=row(d),
        out_shape=jax.ShapeDtypeStruct((bm, t, d), F32),
        scratch_shapes=[pltpu.VMEM((tm, d_ff), BF)],
        compiler_params=_params("parallel", "parallel"),
        name="ffn_mix" if mix is not None else "ffn",
    )(*args)


Q_SCALE = HEAD_DIM ** -0.5 * math.log2(math.e)


def _rope(x, cos, sin_signed):
    w = x.shape[1]
    n = w // LANES
    cos = jnp.concatenate([cos] * n, axis=1)
    sin_signed = jnp.concatenate([sin_signed] * n, axis=1)
    lane = lax.broadcasted_iota(jnp.int32, (1, w), 1)
    first_half = (lane & (HEAD_DIM - 1)) < (HEAD_DIM // 2)
    fwd = pltpu.roll(x, w - HEAD_DIM // 2, axis=1)
    bwd = pltpu.roll(x, HEAD_DIM // 2, axis=1)
    return x * cos + jnp.where(first_half, fwd, bwd) * sin_signed


def _even_in_kernel(*refs, has_rope, is_ctx):
    it = iter(refs)
    x_ref, mod_ref, gain_ref, w_ref, sgu_ref, ws_ref, bs_ref, qg_ref, kg_ref, bd_ref = (next(it) for _ in range(10))
    if has_rope:
        cos_ref, sin_ref = next(it), next(it)
    oa_ref, q_ref, kd_ref, vd_ref = (next(it) for _ in range(4))
    if is_ctx:
        k32_ref, v32_ref = next(it), next(it)

    tm = x_ref.shape[1]
    nck = tm // CHUNK
    x = x_ref[0]
    h = _norm_mod(x, gain_ref[1:2, :], mod_ref[0, 3:4, :], mod_ref[0, 4:5, :]).astype(BF)

    off_q = 2 * A_WIDTH
    kvw = 2 * B_KV * HEAD_DIM
    off_k = off_q + B_WIDTH
    off_v = off_k + kvw
    au = _dot(h, w_ref[:, 0:A_WIDTH])
    av = _dot(h, w_ref[:, A_WIDTH:2 * A_WIDTH])
    q = _dot(h, w_ref[:, off_q:off_q + B_WIDTH])
    k = _dot(h, w_ref[:, off_k:off_k + kvw])
    v = _dot(h, w_ref[:, off_v:off_v + kvw])
    q_ms = _dot((q * q).astype(BF), bd_ref[...])
    k_ms = _dot((k * k).astype(BF), bd_ref[0:kvw, 0:kvw])

    av = _gelu_tanh(av)
    gates = []
    for g in range(A_GROUPS):
        cols = slice(g * A_DIM, (g + 1) * A_DIM)
        blk = av[:, cols]
        ms = jnp.mean(blk * blk, axis=-1, keepdims=True)
        vg = (blk * lax.rsqrt(ms + EPS) * sgu_ref[:, cols]).astype(BF)
        rhs = jnp.concatenate([vg[c * CHUNK:(c + 1) * CHUNK, :] for c in range(nck)], axis=1)
        gates.append(_dot(ws_ref[g], rhs))

    lane = lax.broadcasted_iota(jnp.int32, (1, LANES), 1)
    lo_half = lane < HEAD_DIM
    if has_rope:
        cos = cos_ref[...]
        sin = sin_ref[...]
    q = q * lax.rsqrt(q_ms + EPS) * qg_ref[...]
    if has_rope:
        q = _rope(q, cos, sin)
    q_ref[0] = (q * Q_SCALE).astype(BF)
    k = k * lax.rsqrt(k_ms + EPS) * kg_ref[...]
    if is_ctx:
        k32_ref[0] = jnp.where(lo_half, k[:, 0:LANES], k[:, LANES:2 * LANES])
        v32_ref[0] = jnp.where(lo_half, v[:, 0:LANES], v[:, LANES:2 * LANES])
    if has_rope:
        k = _rope(k, cos, sin)
    kd_ref[0] = k.astype(BF)
    spare = (lax.broadcasted_iota(jnp.int32, (1, kvw), 1) & (LANES - 1)) >= HEAD_DIM
    vd_ref[0] = jnp.where(spare, 1.0, v).astype(BF)
    au = _gelu_tanh(au)
    for g in range(A_GROUPS):
        cols = slice(g * A_DIM, (g + 1) * A_DIM)
        for c in range(nck):
            rows = slice(c * CHUNK, (c + 1) * CHUNK)
            gate = gates[g][:, c * A_DIM:(c + 1) * A_DIM] + bs_ref[g]
            oa_ref[0, rows, cols] = (au[rows, cols] * gate).astype(BF)


def _even_in_call(x, mod, gain, w, sgu, ws, bs, qg, kg, bd, rope, is_ctx):
    bm, t, d = x.shape
    tm = _row_tile(t, ROW_TILE)
    kvw = 2 * B_KV * HEAD_DIM
    row = lambda w_: pl.BlockSpec((1, tm, w_), lambda b, i: (b, i, 0))
    in_specs = [row(d), pl.BlockSpec((1, 3 * N_SUB, d), lambda b, i: (b, 0, 0))]
    in_specs += [_const_spec(a.shape) for a in (gain, w, sgu, ws, bs, qg, kg, bd)]
    args = [x, mod, gain, w, sgu, ws, bs, qg, kg, bd]
    if rope is not None:
        in_specs += [pl.BlockSpec((tm, LANES), lambda b, i: (i, 0))] * 2
        args += list(rope)
    out_specs = [row(A_WIDTH), row(B_WIDTH), row(kvw), row(kvw)]
    out_shape = [jax.ShapeDtypeStruct((bm, t, A_WIDTH), BF), jax.ShapeDtypeStruct((bm, t, B_WIDTH), BF),
                 jax.ShapeDtypeStruct((bm, t, kvw), BF), jax.ShapeDtypeStruct((bm, t, kvw), BF)]
    if is_ctx:
        out_specs += [row(B_KV * HEAD_DIM)] * 2
        out_shape += [jax.ShapeDtypeStruct((bm, t, B_KV * HEAD_DIM), F32)] * 2
    return pl.pallas_call(
        functools.partial(_even_in_kernel, has_rope=rope is not None, is_ctx=is_ctx),
        grid=(bm, t // tm),
        in_specs=in_specs,
        out_specs=out_specs,
        out_shape=out_shape,
        compiler_params=_params("parallel", "parallel"),
        name="even_in",
    )(*args)


ATTN_ROWS = 128
ATTN_SHORT_KEYS = 1024


def _attn_kernel(*refs, n_kv):
    q_ref = refs[0]
    k_refs = refs[1:1 + n_kv]
    v_refs = refs[1 + n_kv:1 + 2 * n_kv]
    o_ref = refs[1 + 2 * n_kv]
    if n_kv > 1:
        k_all, v_all = refs[2 + 2 * n_kv:]

        @pl.when(pl.program_id(1) == 0)
        def _():
            off = 0
            for k_ref, v_ref in zip(k_refs, v_refs):
                n = k_ref.shape[1]
                k_all[off:off + n, :] = k_ref[0]
                v_all[off:off + n, :] = v_ref[0]
                off += n

        n_keys = k_all.shape[0]
        keys = lambda cols: k_all[:, cols]
        vals = lambda cols: v_all[:, cols]
    else:
        n_keys = k_refs[0].shape[1]
        keys = lambda cols: k_refs[0][0, :, cols]
        vals = lambda cols: v_refs[0][0, :, cols]
    tq = q_ref.shape[1]
    lane = lax.broadcasted_iota(jnp.int32, (1, LANES), 1)
    lo_half = lane < HEAD_DIM
    heads_per_kv = B_HEADS // B_KV
    units = [(r, h) for r in range(tq // ATTN_ROWS) for h in range(B_HEADS)]

    def kv_cols(h):
        kv = h // heads_per_kv
        return slice(kv * LANES, (kv + 1) * LANES)

    def scores(unit):
        r, h = unit
        p = h // 2
        qp = q_ref[0, r * ATTN_ROWS:(r + 1) * ATTN_ROWS, p * LANES:(p + 1) * LANES]
        keep = lo_half if h % 2 == 0 else jnp.logical_not(lo_half)
        return _dot_nt(jnp.where(keep, qp, jnp.zeros_like(qp)), keys(kv_cols(h)))

    def weights(s):
        return jnp.exp2(s - jnp.max(s, axis=-1, keepdims=True)).astype(BF)

    def attend(e, unit):
        o2 = _dot(e, vals(kv_cols(unit[1])))
        return o2 / pltpu.roll(o2, HEAD_DIM, axis=1)

    outs = {}
    if n_keys <= ATTN_SHORT_KEYS:
        es = [weights(s) for s in [scores(u) for u in units]]
        for e, unit in zip(es, units):
            outs[unit] = attend(e, unit)
    else:
        s_next = scores(units[0])
        pending = None
        for i, unit in enumerate(units):
            s = s_next
            if i + 1 < len(units):
                s_next = scores(units[i + 1])
            e = weights(s)
            if pending is not None:
                outs[pending[1]] = attend(*pending)
            pending = (e, unit)
        outs[pending[1]] = attend(*pending)
    for r in range(tq // ATTN_ROWS):
        for p in range(B_WIDTH // LANES):
            o_ref[0, r * ATTN_ROWS:(r + 1) * ATTN_ROWS, p * LANES:(p + 1) * LANES] = jnp.where(
                lo_half, outs[(r, 2 * p)], pltpu.roll(outs[(r, 2 * p + 1)], HEAD_DIM, axis=1)).astype(BF)


def _attn_call(q, k_blocks, v_blocks):
    b, t, w = q.shape
    tq = _row_tile(t, ROW_TILE)
    whole = lambda a: pl.BlockSpec((1,) + a.shape[1:], lambda i, j: (i, 0, 0))
    n_kv = len(k_blocks)
    scratch = []
    if n_kv > 1:
        s_all = sum(a.shape[1] for a in k_blocks)
        scratch = [pltpu.VMEM((s_all, k_blocks[0].shape[2]), BF), pltpu.VMEM((s_all, v_blocks[0].shape[2]), BF)]
    return pl.pallas_call(
        functools.partial(_attn_kernel, n_kv=n_kv),
        grid=(b, t // tq),
        in_specs=[pl.BlockSpec((1, tq, w), lambda i, j: (i, j, 0))]
        + [whole(a) for a in k_blocks] + [whole(a) for a in v_blocks],
        out_specs=pl.BlockSpec((1, tq, w), lambda i, j: (i, j, 0)),
        out_shape=jax.ShapeDtypeStruct((b, t, w), BF),
        scratch_shapes=scratch,
        compiler_params=_params("parallel", "arbitrary"),
        name="attn",
    )(q, *k_blocks, *v_blocks)


N_GATE_ROWS = 2 * N_STREAMS


def _log_sigmoid(x):
    return jnp.minimum(x, 0.0) - jnp.log(1.0 + jnp.exp(-jnp.abs(x)))


def _odd_in_kernel(x_ref, mod_ref, gain_ref, w_ref, wt_ref, wgt_ref, bg_ref,
                   q_ref, k_ref, d_ref, vt_ref, ot_ref, gt_ref):
    tm = x_ref.shape[1]
    x = x_ref[0]
    h = _norm_mod(x, gain_ref[1:2, :], mod_ref[0, 3:4, :], mod_ref[0, 4:5, :]).astype(BF)
    w = C_WIDTH
    q_ref[0] = _dot(h, w_ref[:, 0:w]).astype(BF)
    k_ref[0] = (_dot(h, w_ref[:, w:2 * w]) * (C_DK ** -0.5)).astype(BF)
    d_ref[0] = _dot(h, w_ref[:, 2 * w:2 * w + D_WIDTH]).astype(BF)
    vt = _dot_nt(wt_ref[0:w, :], h)
    ot = _sigmoid(_dot_nt(wt_ref[w:2 * w, :], h))
    gt = _dot_nt(wgt_ref[...], h)
    row = lax.broadcasted_iota(jnp.int32, (N_GATE_ROWS, CHUNK), 0)
    for c in range(tm // CHUNK):
        lanes = slice(c * CHUNK, (c + 1) * CHUNK)
        vt_ref[0, c] = vt[:, lanes].astype(BF)
        ot_ref[0, c] = ot[:, lanes].astype(BF)
        g = gt[:, lanes] + bg_ref[...]
        gt_ref[0, c] = jnp.where(row >= N_STREAMS, _log_sigmoid(g), g)


def _odd_in_call(x, mod, gain, w, wt, wgt, bg):
    bm, t, d = x.shape
    tm = _row_tile(t, ROW_TILE)
    nck = tm // CHUNK
    row = lambda w_: pl.BlockSpec((1, tm, w_), lambda b, i: (b, i, 0))
    chunked = lambda r: pl.BlockSpec((1, nck, r, CHUNK), lambda b, i: (b, i, 0, 0))
    return pl.pallas_call(
        _odd_in_kernel,
        grid=(bm, t // tm),
        in_specs=[row(d), pl.BlockSpec((1, 3 * N_SUB, d), lambda b, i: (b, 0, 0))]
        + [_const_spec(a.shape) for a in (gain, w, wt, wgt, bg)],
        out_specs=[row(C_WIDTH), row(C_WIDTH), row(D_WIDTH), chunked(C_WIDTH), chunked(C_WIDTH),
                   chunked(N_GATE_ROWS)],
        out_shape=[jax.ShapeDtypeStruct((bm, t, C_WIDTH), BF), jax.ShapeDtypeStruct((bm, t, C_WIDTH), BF),
                   jax.ShapeDtypeStruct((bm, t, D_WIDTH), BF),
                   jax.ShapeDtypeStruct((bm, t // CHUNK, C_WIDTH, CHUNK), BF),
                   jax.ShapeDtypeStruct((bm, t // CHUNK, C_WIDTH, CHUNK), BF),
                   jax.ShapeDtypeStruct((bm, t // CHUNK, N_GATE_ROWS, CHUNK), F32)],
        compiler_params=_params("parallel", "parallel"),
        name="odd_in",
    )(x, mod, gain, w, wt, wgt, bg)


N_REP = 8
CN_ROWS = C_DV + N_REP


def _split3(x):
    hi = x.astype(BF)
    r1 = x - hi.astype(F32)
    mid = r1.astype(BF)
    lo = (r1 - mid.astype(F32)).astype(BF)
    return hi, mid, lo


def _mlstm_kernel(qf_ref, kf_ref, vtf_ref, otf_ref, gtf_ref, qb_ref, kb_ref, vtb_ref, otb_ref, gtb_ref,
                  c0_ref, n0_ref, m0_ref,
                  hf_ref, hb_ref, c1_ref, n1_ref, m1_ref,
                  cn_s, m_s):
    i = pl.program_id(1)
    nck = qf_ref.shape[1] // CHUNK

    @pl.when(i == 0)
    def _():
        for sl in range(N_STREAMS):
            cn_s[sl, 0:C_DV, :] = c0_ref[0, sl]
            cn_s[sl, C_DV:CN_ROWS, :] = jnp.broadcast_to(n0_ref[0, sl:sl + 1, :], (N_REP, C_DK))
        m_s[...] = m0_ref[0]

    r_i = lax.broadcasted_iota(jnp.int32, (CHUNK, CHUNK), 0)
    c_i = lax.broadcasted_iota(jnp.int32, (CHUNK, CHUNK), 1)
    upper = c_i >= r_i
    lower = c_i <= r_i
    tri_up = jnp.where(upper, 1.0, 0.0).astype(BF)
    tri_lo = jnp.where(lower, 1.0, 0.0).astype(BF)
    eye = jnp.where(c_i == r_i, 1.0, 0.0).astype(BF)
    ones = jnp.ones((CHUNK, CHUNK), BF)
    n_rows = nck * N_STREAMS
    assert n_rows <= CHUNK
    is_fwd = (lax.broadcasted_iota(jnp.int32, (n_rows, CHUNK), 0) & (N_STREAMS - 1)) < C_HEADS
    is_fwd8 = is_fwd[0:N_STREAMS]
    lane = lax.broadcasted_iota(jnp.int32, (n_rows, CHUNK), 1)

    g_i, g_f = [], []
    for c in range(nck):
        gf = gtf_ref[0, c]
        gb = gtb_ref[0, nck - 1 - c]
        g_i.append(jnp.where(is_fwd8, gf[0:N_STREAMS], gb[0:N_STREAMS]))
        g_f.append(jnp.where(is_fwd8, gf[N_STREAMS:N_GATE_ROWS], gb[N_STREAMS:N_GATE_ROWS]))
    g_i = jnp.concatenate(g_i, axis=0)
    g_f = jnp.concatenate(g_f, axis=0)
    parts = _split3(g_f)
    cum = jnp.where(is_fwd, sum(_dot(p, tri_up) for p in parts),
                    sum(_dot(p, tri_lo) for p in parts))
    total = sum(_dot(p, ones) for p in parts)
    r = g_i - cum
    run = r
    sh = 1
    while sh < CHUNK:
        prev = jnp.where(lane >= sh, pltpu.roll(run, sh, axis=1), -jnp.inf)
        nxt = jnp.where(lane < CHUNK - sh, pltpu.roll(run, CHUNK - sh, axis=1), -jnp.inf)
        run = jnp.maximum(run, jnp.where(is_fwd, prev, nxt))
        sh *= 2
    log_s = total - cum + g_i
    ls_max = jnp.max(log_s, axis=-1, keepdims=True)
    m = m_s[...]
    m_old, m_new = [], []
    for c in range(nck):
        rows = slice(c * N_STREAMS, (c + 1) * N_STREAMS)
        m_old.append(m)
        m = jnp.maximum(total[rows] + m, ls_max[rows])
        m_new.append(m)
    m_s[...] = m
    m_old = jnp.concatenate(m_old, axis=0)
    m_new = jnp.concatenate(m_new, axis=0)
    log_inter = cum + m_old
    m_t = jnp.maximum(log_inter, cum + run)
    bm = cum - m_t
    w_prev = jnp.exp(log_inter - m_t)
    e_inv = jnp.exp(-m_t)
    w_src = jnp.exp(log_s - m_new)
    w_keep = jnp.exp(total + m_old - m_new)
    r_cols = jnp.concatenate([r, jnp.zeros((CHUNK - n_rows, CHUNK), F32)], axis=0).T

    row = lambda a, j: a[j:j + 1, :]
    units = []
    for c in range(nck):
        for sl in range(N_STREAMS):
            fwd = sl < C_HEADS
            q_ref, k_ref, vt_ref, ot_ref, h_ref = ((qf_ref, kf_ref, vtf_ref, otf_ref, hf_ref) if fwd
                                                   else (qb_ref, kb_ref, vtb_ref, otb_ref, hb_ref))
            ck = c if fwd else nck - 1 - c
            rows = slice(ck * CHUNK, (ck + 1) * CHUNK)
            cols = slice((sl % C_HEADS) * C_DK, (sl % C_HEADS + 1) * C_DK)
            units.append((c * N_STREAMS + sl, sl, q_ref[0, rows, cols], k_ref[0, rows, cols],
                          vt_ref[0, ck, cols, :], ot_ref[0, ck, cols, :], h_ref, rows, cols))
    s_t = [_dot_nt(k, q) for _, _, q, k, _, _, _, _, _ in units]
    upd = []
    for j, _, _, k, vt, _, _, _, _ in units:
        ws = row(w_src, j)
        lhs = jnp.concatenate([(vt.astype(F32) * ws).astype(BF),
                               jnp.broadcast_to(ws, (N_REP, CHUNK)).astype(BF)], axis=0)
        upd.append(_dot(lhs, k))
    a_t = []
    for (j, sl, *_), s in zip(units, s_t):
        mask = upper if sl < C_HEADS else lower
        a_t.append(jnp.exp(jnp.where(mask, r_cols[:, j:j + 1] + row(bm, j), -jnp.inf)) * s)
    num = [_dot(u[4], a.astype(BF)) for u, a in zip(units, a_t)]
    cn = [cn_s[sl] for sl in range(N_STREAMS)]
    step_units = lambda c: range(c * N_STREAMS, (c + 1) * N_STREAMS)

    def read_state(c):
        out = []
        for idx in step_units(c):
            j, sl, q = units[idx][0:3]
            out.append(_dot_nt(cn[sl].astype(BF), q))
            cn[sl] = row(w_keep, j) * cn[sl] + upd[idx]
        return out

    inter = read_state(0)
    for c in range(nck):
        h_t = []
        for idx, it in zip(step_units(c), inter):
            j, ot = units[idx][0], units[idx][5]
            wp = row(w_prev, j)
            den = jnp.sum(a_t[idx], axis=0, keepdims=True) + wp * it[C_DV:C_DV + 1]
            denom = jnp.maximum(jnp.abs(den), row(e_inv, j))
            h_t.append(((num[idx] + wp * it[0:C_DV]) * (1.0 / denom) * ot.astype(F32)).astype(BF))
        if c + 1 < nck:
            inter = read_state(c + 1)
        for a in range(0, N_STREAMS, 2):
            pair = _dot_nt(eye, jnp.concatenate([h_t[a], h_t[a + 1]], axis=0)).astype(BF)
            h_ref, rows, cols = units[c * N_STREAMS + a][6:9]
            h_ref[0, rows, cols.start:cols.start + 2 * C_DV] = pair
    for sl in range(N_STREAMS):
        cn_s[sl] = cn[sl]

    @pl.when(i == pl.num_programs(1) - 1)
    def _():
        for sl in range(N_STREAMS):
            c1_ref[0, sl] = cn_s[sl, 0:C_DV, :]
            n1_ref[0, sl:sl + 1, :] = cn_s[sl, C_DV:C_DV + 1, :]
        m1_ref[0] = m_s[...]


def _mlstm_call(q, k, vt, ot, gt, c0, n0, m0):
    b, t, w = q.shape
    tb = _row_tile(t, MLSTM_ROW_TILE)
    nt = t // tb
    nck = tb // CHUNK
    fwd = pl.BlockSpec((1, tb, w), lambda i, j: (i, j, 0))
    bwd = pl.BlockSpec((1, tb, w), lambda i, j: (i, nt - 1 - j, 0))
    fwd_t = lambda r: pl.BlockSpec((1, nck, r, CHUNK), lambda i, j: (i, j, 0, 0))
    bwd_t = lambda r: pl.BlockSpec((1, nck, r, CHUNK), lambda i, j: (i, nt - 1 - j, 0, 0))
    st_c = pl.BlockSpec((1, N_STREAMS, C_DV, C_DK), lambda i, j: (i, 0, 0, 0))
    st_n = pl.BlockSpec((1, N_STREAMS, C_DK), lambda i, j: (i, 0, 0))
    st_m = pl.BlockSpec((1, N_STREAMS, CHUNK), lambda i, j: (i, 0, 0))
    return pl.pallas_call(
        _mlstm_kernel,
        grid=(b, nt),
        in_specs=[fwd, fwd, fwd_t(w), fwd_t(w), fwd_t(N_GATE_ROWS), bwd, bwd, bwd_t(w), bwd_t(w),
                  bwd_t(N_GATE_ROWS), st_c, st_n, st_m],
        out_specs=[fwd, bwd, st_c, st_n, st_m],
        out_shape=[jax.ShapeDtypeStruct((b, t, w), BF), jax.ShapeDtypeStruct((b, t, w), BF),
                   jax.ShapeDtypeStruct(c0.shape, F32), jax.ShapeDtypeStruct(n0.shape, F32),
                   jax.ShapeDtypeStruct(m0.shape, F32)],
        scratch_shapes=[pltpu.VMEM((N_STREAMS, CN_ROWS, C_DK), F32), pltpu.VMEM((N_STREAMS, CHUNK), F32)],
        compiler_params=_params("parallel", "arbitrary"),
        name="mlstm",
    )(q, k, vt, ot, gt, q, k, vt, ot, gt, c0, n0, m0)


def _fourier_kernel(d_ref, dft_t_ref, dft_c_ref, o_ref, y_s, *, scale):
    t = d_ref.shape[1]

    @pl.when(pl.program_id(1) == 0)
    def _():
        step = min(t, 512)
        for r in range(t // step):
            rows = slice(r * step, (r + 1) * step)
            for g in range(D_GROUPS):
                cols = slice(g * D_DIM, (g + 1) * D_DIM)
                y = _dot(d_ref[0, rows, cols], dft_c_ref[...])
                y_s[rows, cols] = y[:, 0:D_DIM].astype(BF)
                y_s[t + r * step:t + (r + 1) * step, cols] = y[:, D_DIM:2 * D_DIM].astype(BF)

    o_ref[0] = (_dot(dft_t_ref[...], y_s[...]) * scale).astype(BF)


def _fourier_call(d, dft_t, dft_c):
    b, t, w = d.shape
    tr = _row_tile(t, ROW_TILE)
    return pl.pallas_call(
        functools.partial(_fourier_kernel, scale=float((t * D_DIM) ** -0.5)),
        grid=(b, t // tr),
        in_specs=[
            pl.BlockSpec((1, t, w), lambda i, j: (i, 0, 0)),
            pl.BlockSpec((tr, 2 * t), lambda i, j: (j, 0)),
            _const_spec(dft_c.shape),
        ],
        out_specs=pl.BlockSpec((1, tr, w), lambda i, j: (i, j, 0)),
        out_shape=jax.ShapeDtypeStruct((b, t, w), BF),
        scratch_shapes=[pltpu.VMEM((2 * t, w), BF)],
        compiler_params=_params("parallel", "arbitrary"),
        name="fourier",
    )(d, dft_t, dft_c)


DFT_COLS = 16
DFT_RESIDUES = 8


def _dft_rows_kernel(x_ref, f_ref, zr_ref, zi_ref):
    n1, nc, w = x_ref.shape[1:]
    z = _dot(f_ref[...], x_ref[0].reshape(n1 * nc, w))
    zr_ref[0] = z[0:n1 * nc].astype(BF).reshape(n1, nc, w)
    zi_ref[0] = z[n1 * nc:2 * n1 * nc].astype(BF).reshape(n1, nc, w)


def _dft_rows_call(x4, f1):
    b, n1, n2, w = x4.shape
    spec = pl.BlockSpec((1, n1, DFT_COLS, w), lambda i, j: (i, 0, j, 0))
    return pl.pallas_call(
        _dft_rows_kernel,
        grid=(b, n2 // DFT_COLS),
        in_specs=[spec, _const_spec(f1.shape)],
        out_specs=[spec, spec],
        out_shape=[jax.ShapeDtypeStruct(x4.shape, BF)] * 2,
        compiler_params=_params("parallel", "parallel"),
        name="dft_rows",
    )(x4, f1)


def _dft_cols_kernel(zr_ref, zi_ref, g_ref, c_ref, o_ref, *, scale):
    n2 = GRID_W
    nb = zr_ref.shape[1]
    w = zr_ref.shape[3]
    xr, xi = [], []
    for r in range(nb):
        z = jnp.concatenate([zr_ref[0, r], zi_ref[0, r]], axis=0)
        x = _dot(g_ref[r], z)
        xr.append(x[0:n2])
        xi.append(x[n2:2 * n2])
    xr = jnp.concatenate(xr, axis=0).astype(BF)
    xi = jnp.concatenate(xi, axis=0).astype(BF)
    for g in range(D_GROUPS):
        cols = slice(g * D_DIM, (g + 1) * D_DIM)
        y = _dot(jnp.concatenate([xr[:, cols], xi[:, cols]], axis=1), c_ref[...]) * scale
        for r in range(nb):
            o_ref[0, :, r * w + g * D_DIM:r * w + (g + 1) * D_DIM] = y[r * n2:(r + 1) * n2].astype(BF)


def _dft_cols_call(zr, zi, gmat, cmat, scale):
    b, n1, n2, w = zr.shape
    nb = DFT_RESIDUES
    zspec = pl.BlockSpec((1, nb, n2, w), lambda i, j: (i, j, 0, 0))
    return pl.pallas_call(
        functools.partial(_dft_cols_kernel, scale=scale),
        grid=(b, n1 // nb),
        in_specs=[zspec, zspec, pl.BlockSpec((nb, 2 * n2, 2 * n2), lambda i, j: (j, 0, 0)),
                  _const_spec(cmat.shape)],
        out_specs=pl.BlockSpec((1, n2, nb * w), lambda i, j: (i, 0, j)),
        out_shape=jax.ShapeDtypeStruct((b, n2, n1 * w), BF),
        compiler_params=_params("parallel", "parallel"),
        name="dft_cols",
    )(zr, zi, gmat, cmat)


def _cos_sin(num, den):
    ang = (num % den).astype(F32) * (2.0 * math.pi / den)
    return jnp.cos(ang), jnp.sin(ang)


def _dft_cos_sin(n):
    k = jnp.arange(n, dtype=jnp.int32)
    return _cos_sin(k[:, None] * k[None, :], n)


DENSE_DFT_MAX = 1024


def _fourier(d):
    b, t, w = d.shape
    cc, sc = _dft_cos_sin(D_DIM)
    scale = float((t * D_DIM) ** -0.5)
    if t <= DENSE_DFT_MAX:
        ct, st = _dft_cos_sin(t)
        return _fourier_call(d, jnp.concatenate([ct, -st], axis=1).astype(BF),
                             jnp.concatenate([cc, sc], axis=1).astype(BF))
    n2 = GRID_W
    n1 = t // n2
    assert n1 * n2 == t and n1 % DFT_RESIDUES == 0 and n2 % DFT_COLS == 0
    a1 = jnp.arange(n1, dtype=jnp.int32)
    a2 = jnp.arange(n2, dtype=jnp.int32)
    c1, s1 = _cos_sin(a1[:, None] * a1[None, :], n1)
    idx = jnp.arange(n1 * DFT_COLS, dtype=jnp.int32)
    rep = (idx[:, None] // DFT_COLS == a1[None, :]).astype(BF)
    same_col = (idx[:, None] % DFT_COLS) == (idx[None, :] % DFT_COLS)
    expand = lambda f: jnp.where(same_col, _dot(_dot(rep, f.astype(BF)).astype(BF), rep.T), 0.0)
    f1 = jnp.concatenate([expand(c1), expand(-s1)], axis=0).astype(BF)
    tp = a1[:, None, None] + n1 * a2[None, :, None]
    gc, gs = _cos_sin(tp * a2[None, None, :], t)
    gmat = jnp.concatenate([jnp.concatenate([gc, gs], axis=2),
                            jnp.concatenate([-gs, gc], axis=2)], axis=1).astype(BF)
    cmat = jnp.concatenate([cc, sc], axis=0).astype(BF)
    zr, zi = _dft_rows_call(d.reshape(b, n1, n2, w), f1)
    out = _dft_cols_call(zr, zi, gmat, cmat, scale)
    return out.reshape(b, t, w)


def _rope_tables(t):
    rows = t // GRID_W
    r = jnp.repeat(jnp.arange(rows), GRID_W).astype(F32)
    cidx = jnp.tile(jnp.arange(GRID_W), rows).astype(F32)
    n_freq = HEAD_DIM // 4
    inv = ROPE_THETA ** (-jnp.arange(n_freq, dtype=F32) / n_freq)
    ang = jnp.concatenate([r[:, None] * inv, cidx[:, None] * inv], axis=-1)
    ang = jnp.concatenate([ang] * (2 * LANES // HEAD_DIM), axis=-1)
    lane = jnp.arange(LANES)
    sign = jnp.where((lane % HEAD_DIM) < HEAD_DIM // 2, -1.0, 1.0).astype(F32)
    return jnp.cos(ang), jnp.sin(ang) * sign


def _dup_heads(a, ones=False):
    parts = []
    for h in range(B_KV):
        blk = a[..., h * HEAD_DIM:(h + 1) * HEAD_DIM]
        parts += [blk, jnp.ones_like(blk) if ones else blk]
    return jnp.concatenate(parts, axis=-1)


def kernel(x_prompt, x_sample, c, cache_k, cache_v, state_C, state_n, state_m, c_ctx, norm_gain, w_mod, b_mod,
           ffn_w_in, ffn_w_out, w_in_even, w_out_even, spatial_w, spatial_b, sgu_gain, q_gain, k_gain,
           w_in_odd, b_gate_odd, w_out_odd):
    depth = norm_gain.shape[0]
    batch, seq, d_model = x_prompt.shape
    dec_batch, dec_seq, _ = x_sample.shape

    rows = 1 + dec_batch
    rows_pad = -(-rows // 8) * 8
    c_rows = jnp.concatenate([c_ctx[None, :], c, jnp.zeros((rows_pad - rows, d_model), F32)], axis=0)
    mod_all = _mod_call(c_rows, w_mod, b_mod)
    mod_ctx = mod_all[:, 0:1].reshape(depth, 1, 3 * N_SUB, d_model)
    mod_lat = mod_all[:, 1:rows].reshape(depth, dec_batch, 3 * N_SUB, d_model)

    ffn_in = ffn_w_in.astype(BF)
    ffn_out = ffn_w_out.astype(BF)

    kv = B_KV * HEAD_DIM
    lane = jnp.arange(LANES)
    bd = ((jnp.arange(B_WIDTH)[:, None] // HEAD_DIM) == (jnp.arange(B_WIDTH)[None, :] // HEAD_DIM))
    bd = jnp.where(bd, 1.0 / HEAD_DIM, 0.0).astype(BF)

    def even_weights(e):
        w = w_in_even[e]
        o = 2 * A_WIDTH + B_WIDTH
        w_cat = jnp.concatenate([w[:, :o], _dup_heads(w[:, o:o + kv]), _dup_heads(w[:, o + kv:o + 2 * kv])], axis=1)
        bs = jnp.broadcast_to(spatial_b[e][:, :, None], (A_GROUPS, CHUNK, A_DIM)).astype(F32)
        return dict(
            w=w_cat.astype(BF), sgu=sgu_gain[e].reshape(1, A_WIDTH), ws=spatial_w[e].astype(BF), bs=bs,
            qg=jnp.tile(q_gain[e], B_HEADS)[None, :], kg=jnp.tile(k_gain[e], 2 * B_KV)[None, :],
            wo_a=w_out_even[e][:A_WIDTH].astype(BF), wo_b=w_out_even[e][A_WIDTH:].astype(BF))

    def odd_weights(e):
        w = w_in_odd[e]
        o = 2 * C_HEADS * C_DK + 2 * C_WIDTH
        n_g = 2 * N_STREAMS
        qk = 2 * C_HEADS * C_DK
        w_rows = jnp.concatenate([w[:, :qk], w[:, o + n_g:]], axis=1)
        w_t = w[:, qk:o].T
        perm = jnp.array([d * 2 * C_HEADS + g * C_HEADS + h
                          for g in range(2) for d in range(N_DIR) for h in range(C_HEADS)])
        wg_t = w[:, o:o + n_g][:, perm].T
        bg = jnp.broadcast_to(b_gate_odd[e].reshape(-1)[perm][:, None], (n_g, CHUNK)).astype(F32)
        return dict(w=w_rows.astype(BF), wt=w_t.astype(BF), wgt=wg_t.astype(BF), bg=bg,
                    wo_a=w_out_odd[e][:C_WIDTH].astype(BF), wo_b=w_out_odd[e][C_WIDTH:].astype(BF))

    def run_pass(x, mods, b, t, rope, ctx_even, ctx_odd, is_ctx):
        bm = x.shape[0]
        per_seq = lambda a: a.reshape(b, t, a.shape[-1])
        flat = lambda a: a.reshape(bm, -1, a.shape[-1])
        new_even, new_odd = [], []
        for l in range(depth):
            e = l // 2
            mod, gain = mods[l], norm_gain[l]
            x = _ffn_call(x, mod, gain, ffn_in, ffn_out, l, 0, sub=0)
            if l % 2 == 0:
                p = even_weights(e)
                outs = _even_in_call(x, mod, gain, p["w"], p["sgu"], p["ws"], p["bs"], p["qg"], p["kg"], bd,
                                     rope, is_ctx)
                oa, q, kd, vd = outs[:4]
                k_blocks, v_blocks = [per_seq(kd)], [per_seq(vd)]
                if is_ctx:
                    new_even.append((outs[4], outs[5]))
                else:
                    ck, cv = ctx_even[e]
                    k_blocks.insert(0, _dup_heads(ck.reshape(b, -1, kv)).astype(BF))
                    v_blocks.insert(0, _dup_heads(cv.reshape(b, -1, kv), ones=True).astype(BF))
                ob = _attn_call(per_seq(q), k_blocks, v_blocks)
                mix = ([oa], flat(ob), p["wo_a"], p["wo_b"])
            else:
                p = odd_weights(e)
                q, k, dd, vt, ot, gt = _odd_in_call(x, mod, gain, p["w"], p["wt"], p["wgt"], p["bg"])
                per_seq_t = lambda a: a.reshape((b, t // CHUNK) + a.shape[2:])
                c0, n0, m0 = ctx_odd[e]
                hf, hb, c1, n1, m1 = _mlstm_call(per_seq(q), per_seq(k), per_seq_t(vt), per_seq_t(ot),
                                                 per_seq_t(gt), c0, n0, m0)
                new_odd.append((c1, n1, m1))
                od = _fourier(per_seq(dd))
                mix = ([flat(hf), flat(hb)], flat(od), p["wo_a"], p["wo_b"])
            x = _ffn_call(x, mod, gain, ffn_in, ffn_out, l, 1, sub=2, mix=mix)
        return x, new_even, new_odd

    n_even = (depth + 1) // 2
    n_odd = depth // 2

    zero_state = (jnp.zeros((batch, N_STREAMS, C_DV, C_DK), F32), jnp.zeros((batch, N_STREAMS, C_DK), F32),
                  jnp.zeros((batch, N_STREAMS, CHUNK), F32))
    y, new_even, new_odd = run_pass(x_prompt.reshape(1, batch * seq, d_model), list(mod_ctx), batch, seq,
                                    None, None, [zero_state] * n_odd, True)
    y_prompt = y.reshape(batch, seq, d_model)
    new_cache_k = jnp.stack([k.reshape(batch, seq, B_KV, HEAD_DIM) for k, _ in new_even], axis=1)
    new_cache_v = jnp.stack([v.reshape(batch, seq, B_KV, HEAD_DIM) for _, v in new_even], axis=1)
    new_state_c = jnp.stack([c1.reshape(batch, N_DIR, C_HEADS, C_DV, C_DK) for c1, _, _ in new_odd], axis=1)
    new_state_n = jnp.stack([n1.reshape(batch, N_DIR, C_HEADS, C_DK) for _, n1, _ in new_odd], axis=1)
    new_state_m = jnp.stack([m1[:, :, 0].reshape(batch, N_DIR, C_HEADS) for _, _, m1 in new_odd], axis=1)

    ctx_even = [(cache_k[:, e], cache_v[:, e]) for e in range(n_even)]
    ctx_odd = []
    for e in range(n_odd):
        m0 = jnp.broadcast_to(state_m[:, e].reshape(dec_batch, N_STREAMS, 1), (dec_batch, N_STREAMS, CHUNK))
        ctx_odd.append((state_C[:, e].reshape(dec_batch, N_STREAMS, C_DV, C_DK),
                        state_n[:, e].reshape(dec_batch, N_STREAMS, C_DK), m0))
    y_sample, _, _ = run_pass(x_sample, list(mod_lat), dec_batch, dec_seq, _rope_tables(dec_seq),
                              ctx_even, ctx_odd, False)

    return (y_prompt, y_sample, new_cache_k, new_cache_v, new_state_c, new_state_n, new_state_m)
```
